```python
import jax, jax.numpy as jnp
from jax import lax
import numpy as np

D_MODEL = 4096
BATCH = 4
SEQ = 2048
DEPTH = 4
DEC_BATCH = 8
DEC_SEQ = 1
PAST_LEN = 8192
PAGE_SIZE = 128

MIX_WIDTH = D_MODEL
HEAD_DIM = 128
ATT_WIDTH = MIX_WIDTH // 2
N_ATT_HEADS = ATT_WIDTH // HEAD_DIM
CONV_WIDTH = MIX_WIDTH // 4
N_CONV_GROUPS = CONV_WIDTH // HEAD_DIM
POOL_WIDTH = MIX_WIDTH - ATT_WIDTH - CONV_WIDTH
POOL_WINDOWS = (2, 4, 8, 16)
N_POOL_GROUPS = len(POOL_WINDOWS)
POOL_GROUP = POOL_WIDTH // N_POOL_GROUPS
POOL_PAD = max(POOL_WINDOWS) - 1
CONV_K = 3
D_FF = 256 * ((8 * D_MODEL // 3 + 255) // 256)
IN_WIDTH = 3 * ATT_WIDTH + 3 * CONV_WIDTH + POOL_WIDTH
IN_SPLITS = (ATT_WIDTH, 2 * ATT_WIDTH, 3 * ATT_WIDTH,
             3 * ATT_WIDTH + CONV_WIDTH, 3 * ATT_WIDTH + 2 * CONV_WIDTH, 3 * ATT_WIDTH + 3 * CONV_WIDTH)
SB_BIAS_HI = -6.0
SB_BIAS_LO = -12.0
Q_BLOCK = 128
RMS_EPS = 1e-6

kernel_name = 'hybrid_stickbreak_conv_pool_decoder'


def rms_norm(x, g):
    xf = x.astype(jnp.float32)
    y = xf * lax.rsqrt(jnp.mean(xf * xf, axis=-1, keepdims=True) + RMS_EPS)
    return (y * g.astype(jnp.float32)).astype(x.dtype)


def causal_conv3(u_ext, w):
    t = u_ext.shape[1] - (CONV_K - 1)
    out = w[0] * u_ext[:, 0:t]
    for j in range(1, CONV_K):
        out = out + w[j] * u_ext[:, j:j + t]
    return out


def stick_breaking(q, k, v, sb_bias, q_pos, k_pos):
    z = jnp.einsum('bqhd,bkhd->bhqk', q, k, preferred_element_type=jnp.float32) * (HEAD_DIM ** -0.5)
    z = z + sb_bias.astype(jnp.float32)[None, :, None, None]
    valid = k_pos[None, :] < q_pos[:, None]
    log_1m = jnp.where(valid, jax.nn.log_sigmoid(-z), 0.0)
    rev = lax.cumsum(log_1m, axis=3, reverse=True)
    after = jnp.concatenate([rev[..., 1:], jnp.zeros_like(rev[..., :1])], axis=-1)
    weights = jnp.where(valid, jnp.exp(jax.nn.log_sigmoid(z) + after), 0.0)
    out = jnp.einsum('bhqk,bkhd->bqhd', weights.astype(v.dtype), v, preferred_element_type=jnp.float32)
    return out.astype(q.dtype)


def stick_breaking_attention(q, k_all, v_all, sb_bias, past_len):
    t = q.shape[1]
    outs = []
    for start in range(0, t, Q_BLOCK):
        stop = min(start + Q_BLOCK, t)
        n_keys = past_len + stop
        q_pos = past_len + jnp.arange(start, stop)
        k_pos = jnp.arange(n_keys)
        outs.append(stick_breaking(q[:, start:stop], k_all[:, :n_keys], v_all[:, :n_keys], sb_bias, q_pos, k_pos))
    return jnp.concatenate(outs, axis=1)


def multi_scale_pool(u_ext, pos, w_pool, pool_scale):
    bsz, rows, _ = u_ext.shape
    t = rows - POOL_PAD
    uf = u_ext.astype(jnp.float32)
    cs = jnp.concatenate([jnp.zeros_like(uf[:, :1]), jnp.cumsum(uf, axis=1)], axis=1)
    hi = cs[:, POOL_PAD + 1:]
    parts = []
    for g, win in enumerate(POOL_WINDOWS):
        sl = slice(g * POOL_GROUP, (g + 1) * POOL_GROUP)
        lo = cs[:, POOL_PAD + 1 - win:POOL_PAD + 1 - win + t, sl]
        count = jnp.minimum(win, pos + 1).astype(jnp.float32)[None, :, None]
        parts.append((hi[:, :, sl] - lo) / count)
    diff = jnp.concatenate(parts, axis=-1) - uf[:, POOL_PAD:]
    diff = diff.reshape(bsz, t, N_POOL_GROUPS, POOL_GROUP)
    mixed = jnp.einsum('btgc,gcd->btgd', diff, w_pool.astype(jnp.float32)).reshape(bsz, t, POOL_WIDTH)
    return (mixed * pool_scale.astype(jnp.float32)).astype(u_ext.dtype)


def decoder_layer(x, k_past, v_past, conv_buf, pool_buf, ffn_buf,
                  norm_mix, w_in, sb_bias, conv_w, w_pool, pool_scale, w_out,
                  norm_ffn, w_ffn_gate, ffn_conv_w, ffn_conv_b, w_ffn_up, w_ffn_down):
    bsz, t, _ = x.shape
    past_len = k_past.shape[1]
    pos = past_len + jnp.arange(t)
    xn = rms_norm(x, norm_mix)
    proj = xn @ w_in
    q, k, v, gate_b, gate_c, hc, up = jnp.split(proj, IN_SPLITS, axis=-1)
    q = q.reshape(bsz, t, N_ATT_HEADS, HEAD_DIM)
    k = k.reshape(bsz, t, N_ATT_HEADS, HEAD_DIM)
    v = v.reshape(bsz, t, N_ATT_HEADS, HEAD_DIM)
    k_all = jnp.concatenate([k_past, k], axis=1)
    v_all = jnp.concatenate([v_past, v], axis=1)
    att = stick_breaking_attention(q, k_all, v_all, sb_bias, past_len).reshape(bsz, t, ATT_WIDTH)
    conv_in = jnp.concatenate([conv_buf, gate_c * hc], axis=1)
    conv_out = gate_b * causal_conv3(conv_in, conv_w)
    pool_in = jnp.concatenate([pool_buf, up], axis=1)
    pool_out = multi_scale_pool(pool_in, pos, w_pool, pool_scale)
    mixed = jnp.concatenate([att, conv_out, pool_out], axis=-1)
    h = x + mixed @ w_out
    hn = rms_norm(h, norm_ffn)
    g_in = jnp.concatenate([ffn_buf, hn @ w_ffn_gate], axis=1)
    g = causal_conv3(g_in, ffn_conv_w) + ffn_conv_b
    y = h + (jax.nn.silu(g) * (hn @ w_ffn_up)) @ w_ffn_down
    return (y, k, v, conv_in[:, -(CONV_K - 1):], pool_in[:, -POOL_PAD:], g_in[:, -(CONV_K - 1):])


def setup_inputs(seed: int = 0) -> dict:
    key = jax.random.key(seed)
    ks = jax.random.split(key, 22)
    f32 = jnp.float32
    n_pages = PAST_LEN // PAGE_SIZE
    n_used = DEC_BATCH * n_pages
    n_phys = n_used + n_used // 4

    def nrm(k, shape, scale=1.0):
        return scale * jax.random.normal(k, shape, f32)

    page_table = jax.random.permutation(ks[7], n_phys)[:n_used].reshape(DEC_BATCH, n_pages).astype(jnp.int32)
    sb_bias = jnp.linspace(SB_BIAS_HI, SB_BIAS_LO, N_ATT_HEADS, dtype=f32)[None, :] + nrm(ks[21], (DEPTH, N_ATT_HEADS), 0.1)
    return {
        'x_prompt': nrm(ks[0], (BATCH, SEQ, D_MODEL)),
        'x_sample': nrm(ks[1], (DEC_BATCH, DEC_SEQ, D_MODEL)),
        'cache_k': nrm(ks[2], (DEPTH, n_phys, PAGE_SIZE, N_ATT_HEADS, HEAD_DIM)),
        'cache_v': nrm(ks[3], (DEPTH, n_phys, PAGE_SIZE, N_ATT_HEADS, HEAD_DIM)),
        'state_conv': nrm(ks[4], (DEPTH, DEC_BATCH, CONV_K - 1, CONV_WIDTH)),
        'state_pool': nrm(ks[5], (DEPTH, DEC_BATCH, POOL_PAD, POOL_WIDTH)),
        'state_ffn': nrm(ks[6], (DEPTH, DEC_BATCH, CONV_K - 1, D_FF)),
        'page_table': page_table,
        'norm_mix': 1.0 + nrm(ks[8], (DEPTH, D_MODEL), 0.1),
        'w_in': nrm(ks[9], (DEPTH, D_MODEL, IN_WIDTH), D_MODEL ** -0.5),
        'sb_bias': sb_bias,
        'conv_w': nrm(ks[10], (DEPTH, CONV_K, CONV_WIDTH), CONV_K ** -0.5),
        'w_pool': nrm(ks[11], (DEPTH, N_POOL_GROUPS, POOL_GROUP, POOL_GROUP), POOL_GROUP ** -0.5),
        'pool_scale': 1.0 + nrm(ks[12], (DEPTH, POOL_WIDTH), 0.1),
        'w_out': nrm(ks[13], (DEPTH, MIX_WIDTH, D_MODEL), MIX_WIDTH ** -0.5),
        'norm_ffn': 1.0 + nrm(ks[14], (DEPTH, D_MODEL), 0.1),
        'w_ffn_gate': nrm(ks[15], (DEPTH, D_MODEL, D_FF), D_MODEL ** -0.5),
        'ffn_conv_w': nrm(ks[16], (DEPTH, CONV_K, D_FF), CONV_K ** -0.5),
        'ffn_conv_b': nrm(ks[17], (DEPTH, D_FF), 0.02),
        'w_ffn_up': nrm(ks[18], (DEPTH, D_MODEL, D_FF), D_MODEL ** -0.5),
        'w_ffn_down': nrm(ks[19], (DEPTH, D_FF, D_MODEL), D_FF ** -0.5),
        'norm_final': 1.0 + nrm(ks[20], (D_MODEL,), 0.1),
    }


def reference(x_prompt, x_sample, cache_k, cache_v, state_conv, state_pool, state_ffn, page_table,
              norm_mix, w_in, sb_bias, conv_w, w_pool, pool_scale, w_out, norm_ffn,
              w_ffn_gate, ffn_conv_w, ffn_conv_b, w_ffn_up, w_ffn_down, norm_final):
    n_seq, n_pages = page_table.shape
    past_rows = n_pages * cache_k.shape[2]
    bp = x_prompt.shape[0]
    dt = x_prompt.dtype
    empty_kv = jnp.zeros((bp, 0, N_ATT_HEADS, HEAD_DIM), dt)
    zero_conv = jnp.zeros((bp, CONV_K - 1, CONV_WIDTH), dt)
    zero_pool = jnp.zeros((bp, POOL_PAD, POOL_WIDTH), dt)
    zero_ffn = jnp.zeros((bp, CONV_K - 1, D_FF), dt)
    yp, ys = x_prompt, x_sample
    kp_l, vp_l, cp_l, pp_l, fp_l = [], [], [], [], []
    ks_l, vs_l, cs_l, ps_l, fs_l = [], [], [], [], []
    for l in range(DEPTH):
        lw = (norm_mix[l], w_in[l], sb_bias[l], conv_w[l], w_pool[l], pool_scale[l], w_out[l], norm_ffn[l],
              w_ffn_gate[l], ffn_conv_w[l], ffn_conv_b[l], w_ffn_up[l], w_ffn_down[l])
        yp, kp, vp, cp, pp, fp = decoder_layer(yp, empty_kv, empty_kv, zero_conv, zero_pool, zero_ffn, *lw)
        k_past = cache_k[l][page_table].reshape(n_seq, past_rows, N_ATT_HEADS, HEAD_DIM)
        v_past = cache_v[l][page_table].reshape(n_seq, past_rows, N_ATT_HEADS, HEAD_DIM)
        ys, k_s, v_s, c_s, p_s, f_s = decoder_layer(ys, k_past, v_past, state_conv[l], state_pool[l], state_ffn[l], *lw)
        kp_l.append(kp); vp_l.append(vp); cp_l.append(cp); pp_l.append(pp); fp_l.append(fp)
        ks_l.append(k_s); vs_l.append(v_s); cs_l.append(c_s); ps_l.append(p_s); fs_l.append(f_s)
    yp = rms_norm(yp, norm_final)
    ys = rms_norm(ys, norm_final)
    return (yp, ys, jnp.stack(kp_l), jnp.stack(vp_l), jnp.stack(cp_l), jnp.stack(pp_l), jnp.stack(fp_l),
            jnp.stack(ks_l), jnp.stack(vs_l), jnp.stack(cs_l), jnp.stack(ps_l), jnp.stack(fs_l))
```

```python
import functools

import jax
import jax.numpy as jnp
from jax import lax
from jax.experimental import pallas as pl
from jax.experimental.pallas import tpu as pltpu

HEAD_DIM = 128
CONV_K = 3
POOL_WINDOWS = (2, 4, 8, 16)
POOL_PAD = max(POOL_WINDOWS) - 1
RMS_EPS = 1e-6

F32 = jnp.float32
BF16 = jnp.bfloat16

VMEM_LIMIT_BYTES = 56 * 1024 * 1024
SUBLANES = 8
LANES = 128

FFN_TN = 512
DOWN_TK = 1024


def _cparams(semantics):
    return pltpu.CompilerParams(dimension_semantics=semantics, vmem_limit_bytes=VMEM_LIMIT_BYTES)


def _cdiv(a, b):
    return (a + b - 1) // b


def _norm_body(x_ref, g_ref, o_ref):
    x = x_ref[...]
    ms = jnp.mean(x * x, axis=-1, keepdims=True)
    o_ref[...] = (x * lax.rsqrt(ms + RMS_EPS) * g_ref[...]).astype(o_ref.dtype)


def _rms_norm(x, gains, layer, out_dtype, tm=512):
    m, d = x.shape
    tm = min(tm, m)
    return pl.pallas_call(
        _norm_body,
        grid=(m // tm,),
        in_specs=[pl.BlockSpec((tm, d), lambda i: (i, 0)),
                  pl.BlockSpec((None, 1, d), lambda i: (layer, 0, 0))],
        out_specs=pl.BlockSpec((tm, d), lambda i: (i, 0)),
        out_shape=jax.ShapeDtypeStruct((m, d), out_dtype),
        compiler_params=_cparams(("arbitrary",)),
        name="rmsnorm",
    )(x, gains)


def _cast_weight_panel(w, j, k, *, tk, tn, col0, n_valid, k_valid, mask_n, mask_k):
    if mask_n:
        cols = col0 + j * tn + lax.broadcasted_iota(jnp.int32, w.shape, 1)
        w = jnp.where(cols < n_valid, w, 0.0)
    if mask_k:
        rows = k * tk + lax.broadcasted_iota(jnp.int32, w.shape, 0)
        w = jnp.where(rows < k_valid, w, 0.0)
    return w.astype(BF16)


def _mm_body(*refs, nk, has_res, cast_kw):
    if has_res:
        a_ref, w_ref, r_ref, o_ref, wb_ref, *acc = refs
    else:
        a_ref, w_ref, o_ref, wb_ref, *acc = refs
        r_ref = None
    j, i, k = pl.program_id(0), pl.program_id(1), pl.program_id(2)

    @pl.when(i == 0)
    def _():
        wb_ref[k] = _cast_weight_panel(w_ref[...], j, k, **cast_kw)

    part = jnp.dot(a_ref[...], wb_ref[k], preferred_element_type=F32)

    def finish(total):
        if has_res:
            total = total + r_ref[...]
        o_ref[...] = total.astype(o_ref.dtype)

    if nk == 1:
        finish(part)
    else:
        acc_ref = acc[0]

        @pl.when(k == 0)
        def _():
            acc_ref[...] = part

        @pl.when(jnp.logical_and(k > 0, k < nk - 1))
        def _():
            acc_ref[...] += part

        @pl.when(k == nk - 1)
        def _():
            finish(acc_ref[...] + part)


def _matmul(a, w, layer, *, col0, n_out, out_dtype, residual=None, tm=512, tn=1024, tk=2048):
    m, ka = a.shape
    _, kw, nw = w.shape
    tm, tn, tk = min(tm, m), min(tn, n_out), min(tk, ka)
    assert m % tm == 0 and ka % tk == 0 and n_out % tn == 0 and col0 % tn == 0
    nj, ni, nk = n_out // tn, m // tm, ka // tk
    joff = col0 // tn
    cast_kw = dict(tk=tk, tn=tn, col0=col0, n_valid=nw, k_valid=kw,
                   mask_n=(col0 + n_out > nw), mask_k=(ka > kw))
    in_specs = [
        pl.BlockSpec((tm, tk), lambda j, i, k: (i, k)),
        pl.BlockSpec((None, tk, tn), lambda j, i, k: (layer, jnp.where(i == 0, k, nk - 1), j + joff)),
    ]
    args = [a, w]
    if residual is not None:
        in_specs.append(pl.BlockSpec((tm, tn), lambda j, i, k: (i, j)))
        args.append(residual)
    scratch = [pltpu.VMEM((nk, tk, tn), BF16)]
    if nk > 1:
        scratch.append(pltpu.VMEM((tm, tn), F32))
    return pl.pallas_call(
        functools.partial(_mm_body, nk=nk, has_res=residual is not None, cast_kw=cast_kw),
        grid=(nj, ni, nk),
        in_specs=in_specs,
        out_specs=pl.BlockSpec((tm, tn), lambda j, i, k: (i, j)),
        out_shape=jax.ShapeDtypeStruct((m, n_out), out_dtype),
        scratch_shapes=scratch,
        compiler_params=_cparams(("arbitrary", "arbitrary", "arbitrary")),
        name="matmul",
    )(*args)


def _ffn_body(*refs, nk, tm, tn, f_valid, tiles_per_seq, per_row_state, cast_kw):
    if per_row_state:
        (a_ref, wg_ref, wu_ref, cw_ref, cb_ref, s0_ref, s1_ref,
         act_ref, g_ref, wgb_ref, wub_ref, gs_ref, uacc_ref) = refs
    else:
        (a_ref, wg_ref, wu_ref, cw_ref, cb_ref,
         act_ref, gtail_ref, wgb_ref, wub_ref, gs_ref, uacc_ref) = refs
    j, i, k = pl.program_id(0), pl.program_id(1), pl.program_id(2)
    head = SUBLANES

    @pl.when(i == 0)
    def _():
        wgb_ref[k] = _cast_weight_panel(wg_ref[...], j, k, **cast_kw)
        wub_ref[k] = _cast_weight_panel(wu_ref[...], j, k, **cast_kw)

    a = a_ref[...]
    gpart = jnp.dot(a, wgb_ref[k], preferred_element_type=F32)
    upart = jnp.dot(a, wub_ref[k], preferred_element_type=F32)

    @pl.when(k == 0)
    def _():
        gs_ref[head:, :] = gpart
        uacc_ref[...] = upart

    @pl.when(k > 0)
    def _():
        gs_ref[head:, :] += gpart
        uacc_ref[...] += upart

    @pl.when(k == nk - 1)
    def _():
        g = gs_ref[head:, :]
        cw = cw_ref[...]
        if per_row_state:
            g2, g1 = s0_ref[...], s1_ref[...]
            g_ref[...] = g
        else:
            @pl.when(i % tiles_per_seq == 0)
            def _():
                gs_ref[:head, :] = jnp.zeros((head, tn), F32)

            g1 = gs_ref[head - 1:head - 1 + tm, :]
            g2 = gs_ref[head - 2:head - 2 + tm, :]
            tail = gs_ref[tm:tm + head, :]
            gtail_ref[...] = tail
            gs_ref[:head, :] = tail
        conv = cw[0:1, :] * g2 + cw[1:2, :] * g1 + cw[2:3, :] * g + cb_ref[...]
        act = conv * jax.nn.sigmoid(conv) * uacc_ref[...]
        cols = j * tn + lax.broadcasted_iota(jnp.int32, act.shape, 1)
        act_ref[...] = jnp.where(cols < f_valid, act, 0.0).astype(act_ref.dtype)


def _ffn_gate_up(a, w_gate, w_up, conv_w, conv_b, layer, *, f_pad, seq_len, state=None,
                 tm=1024, tn=512, tk=2048):
    m, ka = a.shape
    _, _, f = w_gate.shape
    per_row_state = state is not None
    tm, tn, tk = min(tm, m, seq_len if not per_row_state else m), min(tn, f_pad), min(tk, ka)
    assert m % tm == 0 and ka % tk == 0 and f_pad % tn == 0
    nj, ni, nk = f_pad // tn, m // tm, ka // tk
    cast_kw = dict(tk=tk, tn=tn, col0=0, n_valid=f, k_valid=ka, mask_n=(f_pad > f), mask_k=False)
    w_spec = pl.BlockSpec((None, tk, tn), lambda j, i, k: (layer, jnp.where(i == 0, k, nk - 1), j))
    in_specs = [
        pl.BlockSpec((tm, tk), lambda j, i, k: (i, k)),
        w_spec, w_spec,
        pl.BlockSpec((None, CONV_K, tn), lambda j, i, k: (layer, 0, j)),
        pl.BlockSpec((None, 1, tn), lambda j, i, k: (layer, 0, j)),
    ]
    args = [a, w_gate, w_up, conv_w, conv_b]
    act_shape = jax.ShapeDtypeStruct((m, f_pad), BF16)
    act_spec = pl.BlockSpec((tm, tn), lambda j, i, k: (i, j))
    if per_row_state:
        tiles_per_seq = 1
        in_specs += [pl.BlockSpec((tm, tn), lambda j, i, k: (i, j))] * 2
        args += list(state)
        out_shape = (act_shape, jax.ShapeDtypeStruct((m, f), F32))
        out_specs = (act_spec, pl.BlockSpec((tm, tn), lambda j, i, k: (i, j)))
    else:
        assert seq_len % tm == 0
        tiles_per_seq = seq_len // tm
        n_seq = m // seq_len
        out_shape = (act_shape, jax.ShapeDtypeStruct((n_seq, SUBLANES, f), F32))
        out_specs = (act_spec, pl.BlockSpec((None, SUBLANES, tn), lambda j, i, k: (i // tiles_per_seq, 0, j)))
    return pl.pallas_call(
        functools.partial(_ffn_body, nk=nk, tm=tm, tn=tn, f_valid=f, tiles_per_seq=tiles_per_seq,
                          per_row_state=per_row_state, cast_kw=cast_kw),
        grid=(nj, ni, nk),
        in_specs=in_specs,
        out_specs=out_specs,
        out_shape=out_shape,
        scratch_shapes=[pltpu.VMEM((nk, tk, tn), BF16), pltpu.VMEM((nk, tk, tn), BF16),
                        pltpu.VMEM((SUBLANES + tm, tn), F32), pltpu.VMEM((tm, tn), F32)],
        compiler_params=_cparams(("arbitrary", "arbitrary", "arbitrary")),
        name="ffn_gate_up",
    )(*args)


def _softplus(z):
    return jnp.maximum(z, 0.0) + jnp.log(1.0 + jnp.exp(-jnp.abs(z)))


def _split_bf16(x):
    hi = x.astype(BF16)
    lo = (x - hi.astype(F32)).astype(BF16)
    return hi, lo


def _attn_prompt_body(bias_ref, q_ref, k_ref, v_ref, tri_ref, o_ref, *, tq, scale):
    h, qi = pl.program_id(1), pl.program_id(2)
    bias = bias_ref[h]
    q = q_ref[...]
    tri = tri_ref[...]

    def block(kb, carry, acc, diag):
        ks = pl.multiple_of(kb * tq, tq)
        kblk = k_ref[pl.ds(ks, tq), :].astype(BF16)
        vblk = v_ref[pl.ds(ks, tq), :].astype(BF16)
        s = lax.dot_general(q, kblk, (((1,), (1,)), ((), ())), preferred_element_type=F32)
        z = s * scale + bias
        sp = _softplus(z)
        log_beta = z - sp
        if diag:
            row = lax.broadcasted_iota(jnp.int32, (tq, tq), 0)
            col = lax.broadcasted_iota(jnp.int32, (tq, tq), 1)
            valid = col < row
            sp = jnp.where(valid, sp, 0.0)
        hi, lo = _split_bf16(sp)
        after = jnp.dot(jnp.concatenate([hi, lo], axis=1), tri, preferred_element_type=F32)
        w = jnp.exp(log_beta - after - carry)
        if diag:
            w = jnp.where(valid, w, 0.0)
        acc = acc + jnp.dot(w.astype(BF16), vblk, preferred_element_type=F32)
        carry = carry + (after[:, 0:1] + sp[:, 0:1])
        return carry, acc

    carry, acc = block(qi, jnp.zeros((tq, 1), F32), jnp.zeros((tq, HEAD_DIM), F32), True)

    def body(it, state):
        return block(qi - 1 - it, state[0], state[1], False)

    carry, acc = lax.fori_loop(0, qi, body, (carry, acc))
    o_ref[...] = acc.astype(o_ref.dtype)


def _suffix_matrix(t):
    j = lax.broadcasted_iota(jnp.int32, (t, t), 0)
    s = lax.broadcasted_iota(jnp.int32, (t, t), 1)
    tri = (j > s).astype(BF16)
    return jnp.concatenate([tri, tri], axis=0)


def _attn_prompt(q, k, v, sb_bias, layer, *, n_seq, seq_len, tq=256):
    m, width = q.shape
    n_heads = width // HEAD_DIM
    tq = min(tq, seq_len)
    nq = seq_len // tq
    scale = HEAD_DIM ** -0.5
    return pl.pallas_call(
        functools.partial(_attn_prompt_body, tq=tq, scale=scale),
        grid=(n_seq, n_heads, nq),
        in_specs=[
            pl.BlockSpec(memory_space=pltpu.SMEM),
            pl.BlockSpec((tq, HEAD_DIM), lambda b, h, qi: (b * nq + qi, h)),
            pl.BlockSpec((seq_len, HEAD_DIM), lambda b, h, qi: (b, h)),
            pl.BlockSpec((seq_len, HEAD_DIM), lambda b, h, qi: (b, h)),
            pl.BlockSpec((2 * tq, tq), lambda b, h, qi: (0, 0)),
        ],
        out_specs=pl.BlockSpec((tq, HEAD_DIM), lambda b, h, qi: (b * nq + qi, h)),
        out_shape=jax.ShapeDtypeStruct((m, width), BF16),
        compiler_params=_cparams(("arbitrary", "arbitrary", "arbitrary")),
        name="attn_prompt",
    )(sb_bias[layer], q, k, v, _suffix_matrix(tq))


def _attn_sample_body(pt_ref, qt_ref, bias_ref, *refs, n_heads, pages_per_step, page, scale):
    k_refs = refs[:pages_per_step]
    v_refs = refs[pages_per_step:2 * pages_per_step]
    tri_ref, o_ref, carry_ref, acc_ref = refs[2 * pages_per_step:]
    c = pl.program_id(1)

    @pl.when(c == 0)
    def _():
        carry_ref[...] = jnp.zeros_like(carry_ref)
        acc_ref[...] = jnp.zeros_like(acc_ref)

    qt = qt_ref[...]
    lane = lax.broadcasted_iota(jnp.int32, (page, LANES), 1)
    out_row = lax.broadcasted_iota(jnp.int32, (n_heads, HEAD_DIM), 0)
    carry = carry_ref[...]
    acc = acc_ref[...]
    for g in range(pages_per_step):
        z = jnp.zeros((page, LANES), F32)
        for h in range(n_heads):
            kh = k_refs[g][pl.ds(h, page, stride=n_heads), :].astype(BF16)
            z = jnp.where(lane == h, jnp.dot(kh, qt, preferred_element_type=F32), z)
        z = z * scale + bias_ref[...]
        sp = _softplus(z)
        hi, lo = _split_bf16(sp)
        both = jnp.dot(tri_ref[...], jnp.concatenate([hi, lo], axis=1), preferred_element_type=F32)
        after = both[:, :LANES] + both[:, LANES:]
        w = jnp.exp(z - sp - after - carry)
        wt = w.T[:n_heads, :].astype(BF16)
        for h in range(n_heads):
            vh = v_refs[g][pl.ds(h, page, stride=n_heads), :].astype(BF16)
            acc = acc + jnp.where(out_row == h, jnp.dot(wt, vh, preferred_element_type=F32), 0.0)
        carry = carry + (after[0:1, :] + sp[0:1, :])
    carry_ref[...] = carry
    acc_ref[...] = acc

    @pl.when(c == pl.num_programs(1) - 1)
    def _():
        o_ref[...] = acc.astype(o_ref.dtype)


def _attn_sample(q, cache_k, cache_v, page_table, sb_bias, layer, pages_per_step=4):
    n_seq, width = q.shape
    n_heads = width // HEAD_DIM
    n_pages = page_table.shape[1]
    page = cache_k.shape[2] // n_heads
    pages_per_step = min(pages_per_step, n_pages)
    assert n_pages % pages_per_step == 0 and n_heads <= LANES
    scale = HEAD_DIM ** -0.5
    qt = jnp.pad(q.reshape(n_seq, n_heads, HEAD_DIM).transpose(0, 2, 1),
                 ((0, 0), (0, 0), (0, LANES - n_heads))).astype(BF16)
    bias = jnp.zeros((1, LANES), F32).at[0, :n_heads].set(sb_bias[layer])
    tri = (lax.broadcasted_iota(jnp.int32, (page, page), 1) > lax.broadcasted_iota(jnp.int32, (page, page), 0)).astype(BF16)

    def kv_spec(g):
        return pl.BlockSpec((None, None, page * n_heads, HEAD_DIM),
                            lambda b, c, pt: (layer, pt[b, n_pages - 1 - (c * pages_per_step + g)], 0, 0))

    kv_specs = [kv_spec(g) for g in range(pages_per_step)]
    out = pl.pallas_call(
        functools.partial(_attn_sample_body, n_heads=n_heads, pages_per_step=pages_per_step, page=page, scale=scale),
        grid_spec=pltpu.PrefetchScalarGridSpec(
            num_scalar_prefetch=1,
            grid=(n_seq, n_pages // pages_per_step),
            in_specs=[pl.BlockSpec((None, HEAD_DIM, LANES), lambda b, c, pt: (b, 0, 0)),
                      pl.BlockSpec((1, LANES), lambda b, c, pt: (0, 0)),
                      *kv_specs, *kv_specs,
                      pl.BlockSpec((page, page), lambda b, c, pt: (0, 0))],
            out_specs=pl.BlockSpec((None, n_heads, HEAD_DIM), lambda b, c, pt: (b, 0, 0)),
            scratch_shapes=[pltpu.VMEM((1, LANES), F32), pltpu.VMEM((n_heads, HEAD_DIM), F32)],
        ),
        out_shape=jax.ShapeDtypeStruct((n_seq, n_heads, HEAD_DIM), BF16),
        compiler_params=_cparams(("arbitrary", "arbitrary")),
        name="attn_sample",
    )(page_table, qt, bias, *([cache_k] * pages_per_step), *([cache_v] * pages_per_step), tri)
    return out.reshape(n_seq, width)


def _pool_mix(window_sums, u, counts, wp_ref, scale):
    group = u.shape[1] // len(POOL_WINDOWS)
    outs = []
    for g in range(len(POOL_WINDOWS)):
        cols = slice(g * group, (g + 1) * group)
        diff = window_sums[g] / counts[g] - u[:, cols]
        outs.append(jnp.dot(diff.astype(BF16), wp_ref[g].astype(BF16), preferred_element_type=F32))
    return jnp.concatenate(outs, axis=1) * scale


def _mixer_prompt_body(gb_ref, gc_ref, hc_ref, u_ref, cw_ref, wp_ref, ps_ref,
                       o_ref, ctail_ref, utail_ref, cs_ref, us_ref, *, tm, past_len):
    t = pl.program_id(1)
    chead, uhead = SUBLANES, 2 * SUBLANES
    cwidth = gc_ref.shape[1]

    @pl.when(t == 0)
    def _():
        cs_ref[:chead, :] = jnp.zeros((chead, cs_ref.shape[1]), F32)
        us_ref[:uhead, :] = jnp.zeros((uhead, us_ref.shape[1]), F32)

    c = gc_ref[...] * hc_ref[...]
    cs_ref[chead:, :] = c
    cw = cw_ref[...]
    conv = (cw[0:1, :] * cs_ref[chead - 2:chead - 2 + tm, :] + cw[1:2, :] * cs_ref[chead - 1:chead - 1 + tm, :]
            + cw[2:3, :] * c)
    o_ref[:, :cwidth] = (gb_ref[...] * conv).astype(o_ref.dtype)

    u = u_ref[...]
    us_ref[uhead:, :] = u
    group = u.shape[1] // len(POOL_WINDOWS)
    pos = past_len + t * tm + lax.broadcasted_iota(jnp.int32, (tm, 1), 0)
    sums, counts = [], []
    for g, win in enumerate(POOL_WINDOWS):
        cols = slice(g * group, (g + 1) * group)
        total = u[:, cols]
        for back in range(1, win):
            total = total + us_ref[uhead - back:uhead - back + tm, cols]
        sums.append(total)
        counts.append(jnp.minimum(win, pos + 1).astype(F32))
    o_ref[:, cwidth:] = _pool_mix(sums, u, counts, wp_ref, ps_ref[...]).astype(o_ref.dtype)

    ctail = cs_ref[tm:tm + chead, :]
    utail = us_ref[tm:tm + uhead, :]
    ctail_ref[...] = ctail
    utail_ref[...] = utail
    cs_ref[:chead, :] = ctail
    us_ref[:uhead, :] = utail


def _mixer_prompt(rest, conv_w, w_pool, pool_scale, layer, *, n_seq, seq_len, tm=256):
    m, w4 = rest.shape
    cw = w4 // 4
    tm = min(tm, seq_len)
    nt = seq_len // tm
    n_groups = len(POOL_WINDOWS)
    group = cw // n_groups

    def col_spec(c):
        return pl.BlockSpec((tm, cw), lambda b, t: (b * nt + t, c))

    return pl.pallas_call(
        functools.partial(_mixer_prompt_body, tm=tm, past_len=0),
        grid=(n_seq, nt),
        in_specs=[col_spec(0), col_spec(1), col_spec(2), col_spec(3),
                  pl.BlockSpec((None, CONV_K, cw), lambda b, t: (layer, 0, 0)),
                  pl.BlockSpec((None, n_groups, group, group), lambda b, t: (layer, 0, 0, 0)),
                  pl.BlockSpec((None, 1, cw), lambda b, t: (layer, 0, 0))],
        out_specs=(pl.BlockSpec((tm, 2 * cw), lambda b, t: (b * nt + t, 0)),
                   pl.BlockSpec((None, SUBLANES, cw), lambda b, t: (b, 0, 0)),
                   pl.BlockSpec((None, 2 * SUBLANES, cw), lambda b, t: (b, 0, 0))),
        out_shape=(jax.ShapeDtypeStruct((m, 2 * cw), BF16),
                   jax.ShapeDtypeStruct((n_seq, SUBLANES, cw), F32),
                   jax.ShapeDtypeStruct((n_seq, 2 * SUBLANES, cw), F32)),
        scratch_shapes=[pltpu.VMEM((SUBLANES + tm, cw), F32), pltpu.VMEM((2 * SUBLANES + tm, cw), F32)],
        compiler_params=_cparams(("arbitrary", "arbitrary")),
        name="mixer_prompt",
    )(rest, rest, rest, rest, conv_w, w_pool, pool_scale)


def _mixer_sample_body(rest_ref, sc_ref, sp_ref, cw_ref, wp_ref, ps_ref, o_ref, c_ref, *, past_len):
    cwidth = cw_ref.shape[1]
    gb = rest_ref[:, 0 * cwidth:1 * cwidth]
    c = rest_ref[:, 1 * cwidth:2 * cwidth] * rest_ref[:, 2 * cwidth:3 * cwidth]
    u = rest_ref[:, 3 * cwidth:4 * cwidth]
    cw = cw_ref[...]
    conv = cw[0:1, :] * sc_ref[0] + cw[1:2, :] * sc_ref[1] + cw[2:3, :] * c
    c_ref[...] = c
    o_ref[:, :cwidth] = (gb * conv).astype(o_ref.dtype)

    group = cwidth // len(POOL_WINDOWS)
    sums, counts = [], []
    for g, win in enumerate(POOL_WINDOWS):
        cols = slice(g * group, (g + 1) * group)
        total = u[:, cols]
        for back in range(1, win):
            total = total + sp_ref[POOL_PAD - back][:, cols]
        sums.append(total)
        counts.append(float(min(win, past_len + 1)))
    o_ref[:, cwidth:] = _pool_mix(sums, u, counts, wp_ref, ps_ref[...]).astype(o_ref.dtype)


def _mixer_sample(rest, conv_state, pool_state, conv_w, w_pool, pool_scale, layer, *, past_len):
    n_seq, w4 = rest.shape
    cw = w4 // 4
    n_groups = len(POOL_WINDOWS)
    group = cw // n_groups
    return pl.pallas_call(
        functools.partial(_mixer_sample_body, past_len=past_len),
        grid=(1,),
        in_specs=[pl.BlockSpec((n_seq, w4), lambda i: (0, 0)),
                  pl.BlockSpec((CONV_K - 1, n_seq, cw), lambda i: (0, 0, 0)),
                  pl.BlockSpec((POOL_PAD, n_seq, cw), lambda i: (0, 0, 0)),
                  pl.BlockSpec((None, CONV_K, cw), lambda i: (layer, 0, 0)),
                  pl.BlockSpec((None, n_groups, group, group), lambda i: (layer, 0, 0, 0)),
                  pl.BlockSpec((None, 1, cw), lambda i: (layer, 0, 0))],
        out_specs=(pl.BlockSpec((n_seq, 2 * cw), lambda i: (0, 0)),
                   pl.BlockSpec((n_seq, cw), lambda i: (0, 0))),
        out_shape=(jax.ShapeDtypeStruct((n_seq, 2 * cw), BF16),
                   jax.ShapeDtypeStruct((n_seq, cw), F32)),
        compiler_params=_cparams(("arbitrary",)),
        name="mixer_sample",
    )(rest, conv_state, pool_state, conv_w, w_pool, pool_scale)


def _layer(x, layer, p, *, n_seq, seq_len, sample=None):
    d = x.shape[1]
    att_w, cw = d // 2, d // 4
    f = p['w_ffn_gate'].shape[2]
    ffn_tn, down_tk = FFN_TN, DOWN_TK
    f_pad = _cdiv(f, down_tk) * down_tk
    assert f_pad - f < ffn_tn

    xn = _rms_norm(x, p['norm_mix'], layer, BF16)
    proj = functools.partial(_matmul, xn, p['w_in'], layer)
    q = proj(col0=0, n_out=att_w, out_dtype=BF16 if sample is None else F32)
    k = proj(col0=att_w, n_out=att_w, out_dtype=F32)
    v = proj(col0=2 * att_w, n_out=att_w, out_dtype=F32)
    rest = proj(col0=3 * att_w, n_out=4 * cw, out_dtype=F32)

    if sample is None:
        att = _attn_prompt(q, k, v, p['sb_bias'], layer, n_seq=n_seq, seq_len=seq_len)
        mix, ctail, utail = _mixer_prompt(rest, p['conv_w'], p['w_pool'], p['pool_scale'], layer,
                                          n_seq=n_seq, seq_len=seq_len)
        conv_state = ctail[:, SUBLANES - (CONV_K - 1):]
        pool_state = utail[:, 2 * SUBLANES - POOL_PAD:]
    else:
        att = _attn_sample(q, sample['cache_k'], sample['cache_v'], sample['page_table'], p['sb_bias'], layer)
        conv_prev, pool_prev = sample['state_conv'][layer], sample['state_pool'][layer]
        mix, c_new = _mixer_sample(rest, conv_prev.transpose(1, 0, 2), pool_prev.transpose(1, 0, 2),
                                   p['conv_w'], p['w_pool'], p['pool_scale'], layer,
                                   past_len=sample['past_len'])
        conv_state = jnp.concatenate([conv_prev[:, 1:], c_new[:, None]], axis=1)
        pool_state = jnp.concatenate([pool_prev[:, 1:], rest[:, None, 3 * cw:]], axis=1)

    mixed = jnp.concatenate([att, mix], axis=1)
    h = _matmul(mixed, p['w_out'], layer, col0=0, n_out=d, out_dtype=F32, residual=x)

    hn = _rms_norm(h, p['norm_ffn'], layer, BF16)
    ffn = functools.partial(_ffn_gate_up, hn, p['w_ffn_gate'], p['w_ffn_up'], p['ffn_conv_w'], p['ffn_conv_b'],
                            layer, f_pad=f_pad, seq_len=seq_len, tn=ffn_tn)
    if sample is None:
        act, gtail = ffn()
        ffn_state = gtail[:, SUBLANES - (CONV_K - 1):]
    else:
        ffn_prev = sample['state_ffn'][layer]
        act, g_new = ffn(state=(ffn_prev[:, 0], ffn_prev[:, 1]))
        ffn_state = jnp.concatenate([ffn_prev[:, 1:], g_new[:, None]], axis=1)
    y = _matmul(act, p['w_ffn_down'], layer, col0=0, n_out=d, out_dtype=F32, residual=h,
                tk=down_tk, tm=512)
    return y, k, v, conv_state, pool_state, ffn_state


def kernel(x_prompt, x_sample, cache_k, cache_v, state_conv, state_pool, state_ffn, page_table, norm_mix, w_in,
           sb_bias, conv_w, w_pool, pool_scale, w_out, norm_ffn, w_ffn_gate, ffn_conv_w, ffn_conv_b, w_ffn_up,
           w_ffn_down, norm_final):
    bp, seq_len, d = x_prompt.shape
    bs, dec_seq, _ = x_sample.shape
    assert dec_seq == 1
    depth, n_phys, page, n_heads, head_dim = cache_k.shape
    assert head_dim == HEAD_DIM
    past_len = page_table.shape[1] * page
    p = dict(
        norm_mix=norm_mix[:, None, :], w_in=w_in, sb_bias=sb_bias, conv_w=conv_w, w_pool=w_pool,
        pool_scale=pool_scale[:, None, :], w_out=w_out, norm_ffn=norm_ffn[:, None, :],
        w_ffn_gate=w_ffn_gate, ffn_conv_w=ffn_conv_w, ffn_conv_b=ffn_conv_b[:, None, :],
        w_ffn_up=w_ffn_up, w_ffn_down=w_ffn_down)
    sample = dict(
        cache_k=cache_k.reshape(depth, n_phys, page * n_heads, head_dim),
        cache_v=cache_v.reshape(depth, n_phys, page * n_heads, head_dim),
        page_table=page_table, state_conv=state_conv, state_pool=state_pool, state_ffn=state_ffn,
        past_len=past_len)

    yp = x_prompt.reshape(bp * seq_len, d)
    ys = x_sample.reshape(bs, d)
    outs_p, outs_s = [], []
    for layer in range(depth):
        yp, *rest_p = _layer(yp, layer, p, n_seq=bp, seq_len=seq_len)
        ys, *rest_s = _layer(ys, layer, p, n_seq=bs, seq_len=1, sample=sample)
        outs_p.append(rest_p)
        outs_s.append(rest_s)
    final_gain = norm_final[None, None, :]
    yp = _rms_norm(yp, final_gain, 0, F32).reshape(bp, seq_len, d)
    ys = _rms_norm(ys, final_gain, 0, F32).reshape(bs, 1, d)

    def stack(outs, idx, shape):
        return jnp.stack([o[idx] for o in outs]).reshape(shape)

    kv_p = (depth, bp, seq_len, n_heads, head_dim)
    kv_s = (depth, bs, 1, n_heads, head_dim)
    return (yp, ys,
            stack(outs_p, 0, kv_p), stack(outs_p, 1, kv_p),
            stack(outs_p, 2, (depth, bp, CONV_K - 1, d // 4)),
            stack(outs_p, 3, (depth, bp, POOL_PAD, d // 4)),
            stack(outs_p, 4, (depth, bp, CONV_K - 1, -1)),
            stack(outs_s, 0, kv_s), stack(outs_s, 1, kv_s),
            stack(outs_s, 2, (depth, bs, CONV_K - 1, d // 4)),
            stack(outs_s, 3, (depth, bs, POOL_PAD, d // 4)),
            stack(outs_s, 4, (depth, bs, CONV_K - 1, -1)))
```

```python
import functools

import jax
import jax.numpy as jnp
from jax import lax
from jax.experimental import pallas as pl
from jax.experimental.pallas import tpu as pltpu

HEAD_DIM = 128
CONV_K = 3
POOL_WINDOWS = (2, 4, 8, 16)
POOL_PAD = max(POOL_WINDOWS) - 1
RMS_EPS = 1e-6
LOG2E = 1.4426950408889634

F32 = jnp.float32
BF16 = jnp.bfloat16

VMEM_LIMIT_BYTES = 56 * 1024 * 1024
SUBLANES = 8
LANES = 128

FFN_PAD = 1024


def _cparams(semantics):
    return pltpu.CompilerParams(dimension_semantics=semantics, vmem_limit_bytes=VMEM_LIMIT_BYTES)


def _cdiv(a, b):
    return (a + b - 1) // b


def _norm_body(x_ref, g_ref, o_ref):
    x = x_ref[...]
    ms = jnp.mean(x * x, axis=-1, keepdims=True)
    o_ref[...] = (x * lax.rsqrt(ms + RMS_EPS) * g_ref[...]).astype(o_ref.dtype)


def _rms_norm(x, gains, layer, out_dtype, tm=512):
    m, d = x.shape
    tm = min(tm, m)
    return pl.pallas_call(
        _norm_body,
        grid=(m // tm,),
        in_specs=[pl.BlockSpec((tm, d), lambda i: (i, 0)),
                  pl.BlockSpec((None, 1, d), lambda i: (layer, 0, 0))],
        out_specs=pl.BlockSpec((tm, d), lambda i: (i, 0)),
        out_shape=jax.ShapeDtypeStruct((m, d), out_dtype),
        compiler_params=_cparams(("arbitrary",)),
        name="rmsnorm",
    )(x, gains)


def _to_bf16_body(w_ref, o_ref, *, tr, k_valid, n_valid, mask_rows):
    w = w_ref[...]
    if mask_rows:
        rows = pl.program_id(1) * tr + lax.broadcasted_iota(jnp.int32, w.shape, 0)
        w = jnp.where(rows < k_valid, w, 0.0)
    o_ref[:, :n_valid] = w.astype(BF16)
    if o_ref.shape[1] > n_valid:
        o_ref[:, n_valid:] = jnp.zeros((tr, o_ref.shape[1] - n_valid), BF16)


def _weights_to_bf16(w, k_pad, n_pad, tr=256):
    depth, k, n = w.shape
    assert k_pad % tr == 0 and k_pad >= k and n_pad >= n
    last_block = (k - 1) // tr
    return pl.pallas_call(
        functools.partial(_to_bf16_body, tr=tr, k_valid=k, n_valid=n, mask_rows=k_pad > k),
        grid=(depth, k_pad // tr),
        in_specs=[pl.BlockSpec((None, tr, n), lambda l, r: (l, jnp.minimum(r, last_block), 0))],
        out_specs=pl.BlockSpec((None, tr, n_pad), lambda l, r: (l, r, 0)),
        out_shape=jax.ShapeDtypeStruct((depth, k_pad, n_pad), BF16),
        compiler_params=_cparams(("arbitrary", "arbitrary")),
        name="weights_to_bf16",
    )(w)


def _mm_body(*refs, has_res):
    if has_res:
        a_ref, w_ref, r_ref, o_ref = refs
    else:
        a_ref, w_ref, o_ref = refs
    acc = jnp.dot(a_ref[...], w_ref[...], preferred_element_type=F32)
    if has_res:
        acc = acc + r_ref[...]
    o_ref[...] = acc.astype(o_ref.dtype)


def _matmul(a, w, layer, *, k0=0, k_len=None, col0, n_out, out_dtype, residual=None, tm, tn):
    m, ka = a.shape
    k_len = ka if k_len is None else k_len
    tm, tn = min(tm, m), min(tn, n_out)
    while col0 % tn != 0:
        tn //= 2
    assert m % tm == 0 and n_out % tn == 0 and tn % LANES == 0 and k0 % k_len == 0
    kblk, joff = k0 // k_len, col0 // tn
    in_specs = [pl.BlockSpec((tm, k_len), lambda j, i: (i, kblk)),
                pl.BlockSpec((None, k_len, tn), lambda j, i: (layer, kblk, j + joff))]
    args = [a, w]
    if residual is not None:
        in_specs.append(pl.BlockSpec((tm, tn), lambda j, i: (i, j)))
        args.append(residual)
    return pl.pallas_call(
        functools.partial(_mm_body, has_res=residual is not None),
        grid=(n_out // tn, m // tm),
        in_specs=in_specs,
        out_specs=pl.BlockSpec((tm, tn), lambda j, i: (i, j)),
        out_shape=jax.ShapeDtypeStruct((m, n_out), out_dtype),
        compiler_params=_cparams(("arbitrary", "arbitrary")),
        name="matmul",
    )(*args)


def _silu_gate(conv, u):
    return conv * pl.reciprocal(1.0 + jnp.exp2(conv * (-LOG2E))) * u


def _ffn_body(*refs, tm, tn, tiles_per_seq, per_row_state):
    if per_row_state:
        a_ref, wg_ref, wu_ref, cw_ref, cb_ref, s0_ref, s1_ref, act_ref, g_ref = refs
    else:
        a_ref, wg_ref, wu_ref, cw_ref, cb_ref, act_ref, gtail_ref, tail_ref = refs
    i = pl.program_id(1)
    if not per_row_state:
        @pl.when(i % tiles_per_seq == 0)
        def _():
            tail_ref[...] = jnp.zeros_like(tail_ref)

    a = a_ref[...]
    g = jnp.dot(a, wg_ref[...], preferred_element_type=F32)
    u = jnp.dot(a, wu_ref[...], preferred_element_type=F32)
    cw = cw_ref[...]
    if per_row_state:
        g2, g1 = s0_ref[...], s1_ref[...]
        g_ref[...] = g
    else:
        prev = tail_ref[...]
        row = lax.broadcasted_iota(jnp.int32, (SUBLANES, tn), 0)
        r1 = pltpu.roll(g, 1, 0)
        r2 = pltpu.roll(g, 2, 0)
        p1 = pltpu.roll(prev, 1, 0)
        p2 = pltpu.roll(prev, 2, 0)
        g1 = jnp.concatenate([jnp.where(row < 1, p1, r1[:SUBLANES]), r1[SUBLANES:]], axis=0)
        g2 = jnp.concatenate([jnp.where(row < 2, p2, r2[:SUBLANES]), r2[SUBLANES:]], axis=0)
        tail = g[tm - SUBLANES:, :]
        gtail_ref[...] = tail
        tail_ref[...] = tail
    conv = cw[0:1, :] * g2 + cw[1:2, :] * g1 + cw[2:3, :] * g + cb_ref[...]
    act_ref[...] = _silu_gate(conv, u).astype(act_ref.dtype)


def _ffn_gate_up(a, w_gate, w_up, conv_w, conv_b, layer, *, seq_len, state=None, tm, tn):
    m, ka = a.shape
    _, _, f = w_gate.shape
    per_row_state = state is not None
    tm, tn = min(tm, m, m if per_row_state else seq_len), min(tn, f)
    assert m % tm == 0 and f % tn == 0
    w_spec = pl.BlockSpec((None, ka, tn), lambda j, i: (layer, 0, j))
    tile_spec = pl.BlockSpec((tm, tn), lambda j, i: (i, j))
    in_specs = [pl.BlockSpec((tm, ka), lambda j, i: (i, 0)), w_spec, w_spec,
                pl.BlockSpec((None, CONV_K, tn), lambda j, i: (layer, 0, j)),
                pl.BlockSpec((None, 1, tn), lambda j, i: (layer, 0, j))]
    args = [a, w_gate, w_up, conv_w, conv_b]
    act_shape = jax.ShapeDtypeStruct((m, f), BF16)
    scratch = []
    if per_row_state:
        tiles_per_seq = 1
        in_specs += [tile_spec, tile_spec]
        args += list(state)
        out_shape = (act_shape, jax.ShapeDtypeStruct((m, f), F32))
        out_specs = (tile_spec, tile_spec)
    else:
        assert seq_len % tm == 0
        tiles_per_seq = seq_len // tm
        out_shape = (act_shape, jax.ShapeDtypeStruct((m // seq_len, SUBLANES, f), F32))
        out_specs = (tile_spec, pl.BlockSpec((None, SUBLANES, tn), lambda j, i: (i // tiles_per_seq, 0, j)))
        scratch = [pltpu.VMEM((SUBLANES, tn), F32)]
    return pl.pallas_call(
        functools.partial(_ffn_body, tm=tm, tn=tn, tiles_per_seq=tiles_per_seq, per_row_state=per_row_state),
        grid=(f // tn, m // tm),
        in_specs=in_specs,
        out_specs=out_specs,
        out_shape=out_shape,
        scratch_shapes=scratch,
        compiler_params=_cparams(("arbitrary", "arbitrary")),
        name="ffn_gate_up",
    )(*args)


def _neg_abs(x):
    bits = lax.bitcast_convert_type(x, jnp.uint32) | jnp.uint32(0x80000000)
    return lax.bitcast_convert_type(bits, F32)


def _softplus2(z2):
    return jnp.maximum(z2, 0.0) + jnp.log2(1.0 + jnp.exp2(_neg_abs(z2)))


def _split_bf16(x):
    hi = x.astype(BF16)
    lo = (x - hi.astype(F32)).astype(BF16)
    return hi, lo


def _attn_prompt_body(bias_ref, q_ref, k_ref, v_ref, tri_ref, o_ref, *, tq, heads, scale2):
    hb, qi = pl.program_id(1), pl.program_id(2)
    tri = tri_ref[...]
    lanes = [slice(c * HEAD_DIM, (c + 1) * HEAD_DIM) for c in range(heads)]
    qs = [q_ref[:, lanes[c]] for c in range(heads)]
    biases = [bias_ref[hb * heads + c] * LOG2E for c in range(heads)]

    def blocks(kb, state, diag):
        ks = pl.multiple_of(kb * tq, tq)
        hs = range(heads)
        s = [lax.dot_general(qs[c], k_ref[pl.ds(ks, tq), lanes[c]].astype(BF16), (((1,), (1,)), ((), ())),
                             preferred_element_type=F32) for c in hs]
        z2 = [s[c] * scale2 + biases[c] for c in hs]
        sp = [_softplus2(z2[c]) for c in hs]
        log_beta = [z2[c] - sp[c] for c in hs]
        if diag:
            row = lax.broadcasted_iota(jnp.int32, (tq, tq), 0)
            col = lax.broadcasted_iota(jnp.int32, (tq, tq), 1)
            valid = col < row
            sp = [jnp.where(valid, sp[c], 0.0) for c in hs]
        split = [_split_bf16(sp[c]) for c in hs]
        after = [jnp.dot(jnp.concatenate(split[c], axis=1), tri, preferred_element_type=F32) for c in hs]
        w = [jnp.exp2(log_beta[c] - after[c] - state[c][0]) for c in hs]
        if diag:
            w = [jnp.where(valid, w[c], 0.0) for c in hs]
        pv = [jnp.dot(w[c].astype(BF16), v_ref[pl.ds(ks, tq), lanes[c]].astype(BF16), preferred_element_type=F32)
              for c in hs]
        return tuple((state[c][0] + (after[c][:, 0:1] + sp[c][:, 0:1]), state[c][1] + pv[c]) for c in hs)

    zero = (jnp.zeros((tq, 1), F32), jnp.zeros((tq, HEAD_DIM), F32))
    state = blocks(qi, (zero,) * heads, True)
    state = lax.fori_loop(0, qi, lambda it, st: blocks(qi - 1 - it, st, False), state)
    for c in range(heads):
        o_ref[:, lanes[c]] = state[c][1].astype(o_ref.dtype)


def _suffix_matrix(t):
    j = lax.broadcasted_iota(jnp.int32, (t, t), 0)
    s = lax.broadcasted_iota(jnp.int32, (t, t), 1)
    tri = (j > s).astype(BF16)
    return jnp.concatenate([tri, tri], axis=0)


def _attn_prompt(q, k, v, sb_bias, layer, *, n_seq, seq_len, tq=256, heads=4):
    m, width = q.shape
    n_heads = width // HEAD_DIM
    tq = min(tq, seq_len)
    nq = seq_len // tq
    assert n_heads % heads == 0
    scale2 = HEAD_DIM ** -0.5 * LOG2E
    hw = heads * HEAD_DIM
    return pl.pallas_call(
        functools.partial(_attn_prompt_body, tq=tq, heads=heads, scale2=scale2),
        grid=(n_seq, n_heads // heads, nq),
        in_specs=[
            pl.BlockSpec(memory_space=pltpu.SMEM),
            pl.BlockSpec((tq, hw), lambda b, h, qi: (b * nq + qi, h)),
            pl.BlockSpec((seq_len, hw), lambda b, h, qi: (b, h)),
            pl.BlockSpec((seq_len, hw), lambda b, h, qi: (b, h)),
            pl.BlockSpec((2 * tq, tq), lambda b, h, qi: (0, 0)),
        ],
        out_specs=pl.BlockSpec((tq, hw), lambda b, h, qi: (b * nq + qi, h)),
        out_shape=jax.ShapeDtypeStruct((m, width), BF16),
        compiler_params=_cparams(("arbitrary", "arbitrary", "arbitrary")),
        name="attn_prompt",
    )(sb_bias[layer], q, k, v, _suffix_matrix(tq))


def _attn_sample_body(pt_ref, qt_ref, bias_ref, *refs, n_heads, pages_per_step, page, scale2):
    k_refs = refs[:pages_per_step]
    v_refs = refs[pages_per_step:2 * pages_per_step]
    tri_ref, o_ref, carry_ref, acc_ref = refs[2 * pages_per_step:]
    c = pl.program_id(1)

    @pl.when(c == 0)
    def _():
        carry_ref[...] = jnp.zeros_like(carry_ref)
        acc_ref[...] = jnp.zeros_like(acc_ref)

    qt = qt_ref[...]
    lane = lax.broadcasted_iota(jnp.int32, (page, LANES), 1)
    out_row = lax.broadcasted_iota(jnp.int32, (n_heads, HEAD_DIM), 0)
    carry = carry_ref[...]
    acc = acc_ref[...]
    for g in range(pages_per_step):
        z = jnp.zeros((page, LANES), F32)
        for h in range(n_heads):
            kh = k_refs[g][pl.ds(h, page, stride=n_heads), :].astype(BF16)
            z = jnp.where(lane == h, jnp.dot(kh, qt, preferred_element_type=F32), z)
        z = z * scale2 + bias_ref[...]
        sp = _softplus2(z)
        hi, lo = _split_bf16(sp)
        both = jnp.dot(tri_ref[...], jnp.concatenate([hi, lo], axis=1), preferred_element_type=F32)
        after = both[:, :LANES] + both[:, LANES:]
        w = jnp.exp2(z - sp - after - carry)
        wt = w.T[:n_heads, :].astype(BF16)
        for h in range(n_heads):
            vh = v_refs[g][pl.ds(h, page, stride=n_heads), :].astype(BF16)
            acc = acc + jnp.where(out_row == h, jnp.dot(wt, vh, preferred_element_type=F32), 0.0)
        carry = carry + (after[0:1, :] + sp[0:1, :])
    carry_ref[...] = carry
    acc_ref[...] = acc

    @pl.when(c == pl.num_programs(1) - 1)
    def _():
        o_ref[...] = acc.astype(o_ref.dtype)


def _attn_sample(q, cache_k, cache_v, page_table, sb_bias, layer, pages_per_step=4):
    n_seq, width = q.shape
    n_heads = width // HEAD_DIM
    n_pages = page_table.shape[1]
    page = cache_k.shape[2] // n_heads
    pages_per_step = min(pages_per_step, n_pages)
    assert n_pages % pages_per_step == 0 and n_heads <= LANES
    scale2 = HEAD_DIM ** -0.5 * LOG2E
    qt = jnp.pad(q.reshape(n_seq, n_heads, HEAD_DIM).transpose(0, 2, 1),
                 ((0, 0), (0, 0), (0, LANES - n_heads))).astype(BF16)
    bias = jnp.zeros((1, LANES), F32).at[0, :n_heads].set(sb_bias[layer] * LOG2E)
    tri = (lax.broadcasted_iota(jnp.int32, (page, page), 1) > lax.broadcasted_iota(jnp.int32, (page, page), 0)).astype(BF16)

    def kv_spec(g):
        return pl.BlockSpec((None, None, page * n_heads, HEAD_DIM),
                            lambda b, c, pt: (layer, pt[b, n_pages - 1 - (c * pages_per_step + g)], 0, 0))

    kv_specs = [kv_spec(g) for g in range(pages_per_step)]
    out = pl.pallas_call(
        functools.partial(_attn_sample_body, n_heads=n_heads, pages_per_step=pages_per_step, page=page, scale2=scale2),
        grid_spec=pltpu.PrefetchScalarGridSpec(
            num_scalar_prefetch=1,
            grid=(n_seq, n_pages // pages_per_step),
            in_specs=[pl.BlockSpec((None, HEAD_DIM, LANES), lambda b, c, pt: (b, 0, 0)),
                      pl.BlockSpec((1, LANES), lambda b, c, pt: (0, 0)),
                      *kv_specs, *kv_specs,
                      pl.BlockSpec((page, page), lambda b, c, pt: (0, 0))],
            out_specs=pl.BlockSpec((None, n_heads, HEAD_DIM), lambda b, c, pt: (b, 0, 0)),
            scratch_shapes=[pltpu.VMEM((1, LANES), F32), pltpu.VMEM((n_heads, HEAD_DIM), F32)],
        ),
        out_shape=jax.ShapeDtypeStruct((n_seq, n_heads, HEAD_DIM), BF16),
        compiler_params=_cparams(("arbitrary", "arbitrary")),
        name="attn_sample",
    )(page_table, qt, bias, *([cache_k] * pages_per_step), *([cache_v] * pages_per_step), tri)
    return out.reshape(n_seq, width)


def _pool_mix(window_sums, u, counts, wp_ref, scale):
    group = u.shape[1] // len(POOL_WINDOWS)
    outs = []
    for g in range(len(POOL_WINDOWS)):
        cols = slice(g * group, (g + 1) * group)
        diff = window_sums[g] / counts[g] - u[:, cols]
        outs.append(jnp.dot(diff.astype(BF16), wp_ref[g].astype(BF16), preferred_element_type=F32))
    return jnp.concatenate(outs, axis=1) * scale


def _mixer_prompt_body(gb_ref, gc_ref, hc_ref, u_ref, cw_ref, wp_ref, ps_ref,
                       o_ref, ctail_ref, utail_ref, cs_ref, us_ref, *, tm, past_len):
    t = pl.program_id(1)
    chead, uhead = SUBLANES, 2 * SUBLANES
    cwidth = gc_ref.shape[1]

    @pl.when(t == 0)
    def _():
        cs_ref[:chead, :] = jnp.zeros((chead, cs_ref.shape[1]), F32)
        us_ref[:uhead, :] = jnp.zeros((uhead, us_ref.shape[1]), F32)

    c = gc_ref[...] * hc_ref[...]
    cs_ref[chead:, :] = c
    cw = cw_ref[...]
    conv = (cw[0:1, :] * cs_ref[chead - 2:chead - 2 + tm, :] + cw[1:2, :] * cs_ref[chead - 1:chead - 1 + tm, :]
            + cw[2:3, :] * c)
    o_ref[:, :cwidth] = (gb_ref[...] * conv).astype(o_ref.dtype)

    u = u_ref[...]
    us_ref[uhead:, :] = u
    group = u.shape[1] // len(POOL_WINDOWS)
    pos = past_len + t * tm + lax.broadcasted_iota(jnp.int32, (tm, 1), 0)
    sums, counts = [], []
    for g, win in enumerate(POOL_WINDOWS):
        cols = slice(g * group, (g + 1) * group)
        total = u[:, cols]
        for back in range(1, win):
            total = total + us_ref[uhead - back:uhead - back + tm, cols]
        sums.append(total)
        counts.append(jnp.minimum(win, pos + 1).astype(F32))
    o_ref[:, cwidth:] = _pool_mix(sums, u, counts, wp_ref, ps_ref[...]).astype(o_ref.dtype)

    ctail = cs_ref[tm:tm + chead, :]
    utail = us_ref[tm:tm + uhead, :]
    ctail_ref[...] = ctail
    utail_ref[...] = utail
    cs_ref[:chead, :] = ctail
    us_ref[:uhead, :] = utail


def _mixer_prompt(rest, conv_w, w_pool, pool_scale, layer, *, n_seq, seq_len, tm=256):
    m, w4 = rest.shape
    cw = w4 // 4
    tm = min(tm, seq_len)
    nt = seq_len // tm
    n_groups = len(POOL_WINDOWS)
    group = cw // n_groups

    def col_spec(c):
        return pl.BlockSpec((tm, cw), lambda b, t: (b * nt + t, c))

    return pl.pallas_call(
        functools.partial(_mixer_prompt_body, tm=tm, past_len=0),
        grid=(n_seq, nt),
        in_specs=[col_spec(0), col_spec(1), col_spec(2), col_spec(3),
                  pl.BlockSpec((None, CONV_K, cw), lambda b, t: (layer, 0, 0)),
                  pl.BlockSpec((None, n_groups, group, group), lambda b, t: (layer, 0, 0, 0)),
                  pl.BlockSpec((None, 1, cw), lambda b, t: (layer, 0, 0))],
        out_specs=(pl.BlockSpec((tm, 2 * cw), lambda b, t: (b * nt + t, 0)),
                   pl.BlockSpec((None, SUBLANES, cw), lambda b, t: (b, 0, 0)),
                   pl.BlockSpec((None, 2 * SUBLANES, cw), lambda b, t: (b, 0, 0))),
        out_shape=(jax.ShapeDtypeStruct((m, 2 * cw), BF16),
                   jax.ShapeDtypeStruct((n_seq, SUBLANES, cw), F32),
                   jax.ShapeDtypeStruct((n_seq, 2 * SUBLANES, cw), F32)),
        scratch_shapes=[pltpu.VMEM((SUBLANES + tm, cw), F32), pltpu.VMEM((2 * SUBLANES + tm, cw), F32)],
        compiler_params=_cparams(("arbitrary", "arbitrary")),
        name="mixer_prompt",
    )(rest, rest, rest, rest, conv_w, w_pool, pool_scale)


def _mixer_sample_body(rest_ref, sc_ref, sp_ref, cw_ref, wp_ref, ps_ref, o_ref, c_ref, *, past_len):
    cwidth = cw_ref.shape[1]
    gb = rest_ref[:, 0 * cwidth:1 * cwidth]
    c = rest_ref[:, 1 * cwidth:2 * cwidth] * rest_ref[:, 2 * cwidth:3 * cwidth]
    u = rest_ref[:, 3 * cwidth:4 * cwidth]
    cw = cw_ref[...]
    conv = cw[0:1, :] * sc_ref[0] + cw[1:2, :] * sc_ref[1] + cw[2:3, :] * c
    c_ref[...] = c
    o_ref[:, :cwidth] = (gb * conv).astype(o_ref.dtype)

    group = cwidth // len(POOL_WINDOWS)
    sums, counts = [], []
    for g, win in enumerate(POOL_WINDOWS):
        cols = slice(g * group, (g + 1) * group)
        total = u[:, cols]
        for back in range(1, win):
            total = total + sp_ref[POOL_PAD - back][:, cols]
        sums.append(total)
        counts.append(float(min(win, past_len + 1)))
    o_ref[:, cwidth:] = _pool_mix(sums, u, counts, wp_ref, ps_ref[...]).astype(o_ref.dtype)


def _mixer_sample(rest, conv_state, pool_state, conv_w, w_pool, pool_scale, layer, *, past_len):
    n_seq, w4 = rest.shape
    cw = w4 // 4
    n_groups = len(POOL_WINDOWS)
    group = cw // n_groups
    return pl.pallas_call(
        functools.partial(_mixer_sample_body, past_len=past_len),
        grid=(1,),
        in_specs=[pl.BlockSpec((n_seq, w4), lambda i: (0, 0)),
                  pl.BlockSpec((CONV_K - 1, n_seq, cw), lambda i: (0, 0, 0)),
                  pl.BlockSpec((POOL_PAD, n_seq, cw), lambda i: (0, 0, 0)),
                  pl.BlockSpec((None, CONV_K, cw), lambda i: (layer, 0, 0)),
                  pl.BlockSpec((None, n_groups, group, group), lambda i: (layer, 0, 0, 0)),
                  pl.BlockSpec((None, 1, cw), lambda i: (layer, 0, 0))],
        out_specs=(pl.BlockSpec((n_seq, 2 * cw), lambda i: (0, 0)),
                   pl.BlockSpec((n_seq, cw), lambda i: (0, 0))),
        out_shape=(jax.ShapeDtypeStruct((n_seq, 2 * cw), BF16),
                   jax.ShapeDtypeStruct((n_seq, cw), F32)),
        compiler_params=_cparams(("arbitrary",)),
        name="mixer_sample",
    )(rest, conv_state, pool_state, conv_w, w_pool, pool_scale)


def _layer(x, layer, p, *, n_seq, seq_len, sample=None):
    d = x.shape[1]
    att_w, cw = d // 2, d // 4
    f, f_pad = p['f'], p['w_ffn_gate'].shape[2]
    prompt = sample is None
    tiles = dict(tm=1024, tn=1024) if prompt else dict(tm=SUBLANES, tn=2048)
    res_tiles = dict(tm=512, tn=1024) if prompt else dict(tm=SUBLANES, tn=1024)

    xn = _rms_norm(x, p['norm_mix'], layer, BF16)
    proj = functools.partial(_matmul, xn, p['w_in'], layer, **tiles)
    q = proj(col0=0, n_out=att_w, out_dtype=BF16 if prompt else F32)
    k = proj(col0=att_w, n_out=att_w, out_dtype=F32)
    v = proj(col0=2 * att_w, n_out=att_w, out_dtype=F32)
    rest = proj(col0=3 * att_w, n_out=4 * cw, out_dtype=F32)

    if prompt:
        att = _attn_prompt(q, k, v, p['sb_bias'], layer, n_seq=n_seq, seq_len=seq_len)
        mix, ctail, utail = _mixer_prompt(rest, p['conv_w'], p['w_pool'], p['pool_scale'], layer,
                                          n_seq=n_seq, seq_len=seq_len)
        conv_state = ctail[:, SUBLANES - (CONV_K - 1):]
        pool_state = utail[:, 2 * SUBLANES - POOL_PAD:]
    else:
        att = _attn_sample(q, sample['cache_k'], sample['cache_v'], sample['page_table'], p['sb_bias'], layer)
        conv_prev, pool_prev = sample['state_conv'][layer], sample['state_pool'][layer]
        mix, c_new = _mixer_sample(rest, conv_prev.transpose(1, 0, 2), pool_prev.transpose(1, 0, 2),
                                   p['conv_w'], p['w_pool'], p['pool_scale'], layer,
                                   past_len=sample['past_len'])
        conv_state = jnp.concatenate([conv_prev[:, 1:], c_new[:, None]], axis=1)
        pool_state = jnp.concatenate([pool_prev[:, 1:], rest[:, None, 3 * cw:]], axis=1)

    mixed = jnp.concatenate([att, mix], axis=1)
    h = _matmul(mixed, p['w_out'], layer, col0=0, n_out=d, out_dtype=F32, residual=x, **res_tiles)

    hn = _rms_norm(h, p['norm_ffn'], layer, BF16)
    ffn = functools.partial(_ffn_gate_up, hn, p['w_ffn_gate'], p['w_ffn_up'], p['ffn_conv_w'], p['ffn_conv_b'],
                            layer, seq_len=seq_len)
    if prompt:
        act, gtail = ffn(tm=1024, tn=512)
        ffn_state = gtail[:, SUBLANES - (CONV_K - 1):, :f]
    else:
        ffn_prev = jnp.pad(sample['state_ffn'][layer], ((0, 0), (0, 0), (0, f_pad - f)))
        act, g_new = ffn(state=(ffn_prev[:, 0], ffn_prev[:, 1]), tm=SUBLANES, tn=1024)
        ffn_state = jnp.concatenate([ffn_prev[:, 1:], g_new[:, None]], axis=1)[:, :, :f]
    half = f_pad // 2
    down = functools.partial(_matmul, act, p['w_ffn_down'], layer, k_len=half, col0=0, n_out=d, out_dtype=F32,
                             **res_tiles)
    y = down(k0=half, residual=down(k0=0, residual=h))
    return y, k, v, conv_state, pool_state, ffn_state


def kernel(x_prompt, x_sample, cache_k, cache_v, state_conv, state_pool, state_ffn, page_table, norm_mix, w_in,
           sb_bias, conv_w, w_pool, pool_scale, w_out, norm_ffn, w_ffn_gate, ffn_conv_w, ffn_conv_b, w_ffn_up,
           w_ffn_down, norm_final):
    bp, seq_len, d = x_prompt.shape
    bs, dec_seq, _ = x_sample.shape
    assert dec_seq == 1
    depth, n_phys, page, n_heads, head_dim = cache_k.shape
    assert head_dim == HEAD_DIM
    past_len = page_table.shape[1] * page
    f = w_ffn_gate.shape[2]
    f_pad = _cdiv(f, FFN_PAD) * FFN_PAD
    pad_f = ((0, 0), (0, 0), (0, f_pad - f))
    p = dict(
        f=f, norm_mix=norm_mix[:, None, :], sb_bias=sb_bias, conv_w=conv_w, w_pool=w_pool,
        pool_scale=pool_scale[:, None, :], norm_ffn=norm_ffn[:, None, :],
        ffn_conv_w=jnp.pad(ffn_conv_w, pad_f), ffn_conv_b=jnp.pad(ffn_conv_b[:, None, :], pad_f),
        w_in=_weights_to_bf16(w_in, d, w_in.shape[2]), w_out=_weights_to_bf16(w_out, d, d),
        w_ffn_gate=_weights_to_bf16(w_ffn_gate, d, f_pad), w_ffn_up=_weights_to_bf16(w_ffn_up, d, f_pad),
        w_ffn_down=_weights_to_bf16(w_ffn_down, f_pad, d))
    sample = dict(
        cache_k=cache_k.reshape(depth, n_phys, page * n_heads, head_dim),
        cache_v=cache_v.reshape(depth, n_phys, page * n_heads, head_dim),
        page_table=page_table, state_conv=state_conv, state_pool=state_pool, state_ffn=state_ffn,
        past_len=past_len)

    yp = x_prompt.reshape(bp * seq_len, d)
    ys = x_sample.reshape(bs, d)
    outs_p, outs_s = [], []
    for layer in range(depth):
        yp, *rest_p = _layer(yp, layer, p, n_seq=bp, seq_len=seq_len)
        ys, *rest_s = _layer(ys, layer, p, n_seq=bs, seq_len=1, sample=sample)
        outs_p.append(rest_p)
        outs_s.append(rest_s)
    final_gain = norm_final[None, None, :]
    yp = _rms_norm(yp, final_gain, 0, F32).reshape(bp, seq_len, d)
    ys = _rms_norm(ys, final_gain, 0, F32).reshape(bs, 1, d)

    def stack(outs, idx, shape):
        return jnp.stack([o[idx] for o in outs]).reshape(shape)

    kv_p = (depth, bp, seq_len, n_heads, head_dim)
    kv_s = (depth, bs, 1, n_heads, head_dim)
    return (yp, ys,
            stack(outs_p, 0, kv_p), stack(outs_p, 1, kv_p),
            stack(outs_p, 2, (depth, bp, CONV_K - 1, d // 4)),
            stack(outs_p, 3, (depth, bp, POOL_PAD, d // 4)),
            stack(outs_p, 4, (depth, bp, CONV_K - 1, -1)),
            stack(outs_s, 0, kv_s), stack(outs_s, 1, kv_s),
            stack(outs_s, 2, (depth, bs, CONV_K - 1, d // 4)),
            stack(outs_s, 3, (depth, bs, POOL_PAD, d // 4)),
            stack(outs_s, 4, (depth, bs, CONV_K - 1, -1)))
```

```python
import functools

import jax
import jax.numpy as jnp
from jax import lax
from jax.experimental import pallas as pl
from jax.experimental.pallas import tpu as pltpu

HEAD_DIM = 128
CONV_K = 3
POOL_WINDOWS = (2, 4, 8, 16)
POOL_PAD = max(POOL_WINDOWS) - 1
RMS_EPS = 1e-6
LOG2E = 1.4426950408889634
SCALE2 = HEAD_DIM ** -0.5 * LOG2E

F32 = jnp.float32
BF16 = jnp.bfloat16

VMEM_LIMIT_BYTES = 56 * 1024 * 1024
SUBLANES = 8
LANES = 128

FFN_PAD = 1024


def _cparams(semantics):
    return pltpu.CompilerParams(dimension_semantics=semantics, vmem_limit_bytes=VMEM_LIMIT_BYTES)


def _cdiv(a, b):
    return (a + b - 1) // b


def _norm_body(x_ref, g_ref, o_ref):
    x = x_ref[...]
    ms = jnp.mean(x * x, axis=-1, keepdims=True)
    o_ref[...] = (x * lax.rsqrt(ms + RMS_EPS) * g_ref[...]).astype(o_ref.dtype)


def _rms_norm(x, gains, layer, out_dtype, tm=512):
    m, d = x.shape
    tm = min(tm, m)
    return pl.pallas_call(
        _norm_body,
        grid=(m // tm,),
        in_specs=[pl.BlockSpec((tm, d), lambda i: (i, 0)),
                  pl.BlockSpec((None, 1, d), lambda i: (layer, 0, 0))],
        out_specs=pl.BlockSpec((tm, d), lambda i: (i, 0)),
        out_shape=jax.ShapeDtypeStruct((m, d), out_dtype),
        compiler_params=_cparams(("arbitrary",)),
        name="rmsnorm",
    )(x, gains)


def _mm_body(*refs, n_a, has_res):
    a_refs, (w_ref, *rest) = refs[:n_a], refs[n_a:]
    r_ref = rest[0] if has_res else None
    o_ref = rest[-1]
    acc, k0 = None, 0
    for a_ref in a_refs:
        k1 = k0 + a_ref.shape[1]
        part = jnp.dot(a_ref[...], w_ref[k0:k1, :], preferred_element_type=F32)
        acc, k0 = part if acc is None else acc + part, k1
    if has_res:
        acc = acc + r_ref[...]
    o_ref[...] = acc.astype(o_ref.dtype)


def _matmul(a, w, layer, *, k0=0, w_k0=None, k_len=None, col0, n_out, out_dtype, residual=None, stack=None,
            tm, tn):
    a_list = list(a) if isinstance(a, (tuple, list)) else [a]
    m = a_list[0].shape[0]
    ka = sum(x.shape[1] for x in a_list)
    k_len = ka if k_len is None else k_len
    assert len(a_list) == 1 or k_len == ka
    tm, tn = min(tm, m), min(tn, n_out)
    while col0 % tn != 0:
        tn //= 2
    assert m % tm == 0 and n_out % tn == 0 and tn % LANES == 0 and k0 % k_len == 0
    w_k0 = k0 if w_k0 is None else w_k0
    assert w_k0 % k_len == 0
    kblk, w_kblk, joff = k0 // k_len, w_k0 // k_len, col0 // tn
    if len(a_list) == 1:
        in_specs = [pl.BlockSpec((tm, k_len), lambda j, i: (i, kblk))]
    else:
        in_specs = [pl.BlockSpec((tm, x.shape[1]), lambda j, i: (i, 0)) for x in a_list]
    in_specs.append(pl.BlockSpec((None, k_len, tn), lambda j, i: (layer, w_kblk, j + joff)))
    args = a_list + [w]
    if residual is not None:
        in_specs.append(pl.BlockSpec((tm, tn), lambda j, i: (i, j)))
        args.append(residual)
    aliases = {}
    if stack is None:
        out_spec = pl.BlockSpec((tm, tn), lambda j, i: (i, j))
        out_shape = jax.ShapeDtypeStruct((m, n_out), out_dtype)
    else:
        depth, slot, prev = stack
        out_spec = pl.BlockSpec((None, tm, tn), lambda j, i: (slot, i, j))
        out_shape = jax.ShapeDtypeStruct((depth, m, n_out), out_dtype)
        if prev is not None:
            in_specs.append(pl.BlockSpec(memory_space=pl.ANY))
            args.append(prev)
            aliases = {len(args) - 1: 0}
    body = functools.partial(_mm_body, n_a=len(a_list), has_res=residual is not None)
    if aliases:
        inner = body
        body = lambda *refs: inner(*refs[:-2], refs[-1])
    return pl.pallas_call(
        body,
        grid=(n_out // tn, m // tm),
        in_specs=in_specs,
        out_specs=out_spec,
        out_shape=out_shape,
        input_output_aliases=aliases,
        compiler_params=_cparams(("arbitrary", "arbitrary")),
        name="matmul",
    )(*args)


def _cast_weight_block(w, j, *, tn, n_valid, k_valid):
    if n_valid is not None:
        cols = j * tn + lax.broadcasted_iota(jnp.int32, w.shape, 1)
        w = jnp.where(cols < n_valid, w, 0.0)
    if k_valid is not None:
        rows = lax.broadcasted_iota(jnp.int32, w.shape, 0)
        w = jnp.where(rows < k_valid, w, 0.0)
    return w.astype(BF16)


def _mm_cast_body(*refs, has_res, cast_kw):
    if has_res:
        a_ref, w_ref, r_ref, o_ref, wb_ref = refs
    else:
        a_ref, w_ref, o_ref, wb_ref = refs
    wb = _cast_weight_block(w_ref[...], pl.program_id(0), **cast_kw)
    wb_ref[...] = wb
    acc = jnp.dot(a_ref[...], wb, preferred_element_type=F32)
    if has_res:
        acc = acc + r_ref[...]
    o_ref[...] = acc.astype(o_ref.dtype)


def _edge_kw(tn, col0, n_out, n_w, k0, k_len, k_w):
    return dict(tn=tn, n_valid=(n_w - col0) if col0 + n_out > n_w else None,
                k_valid=(k_w - k0) if k0 + k_len > k_w else None)


def _matmul_cast(a, w, layer, *, k0=0, k_len=None, col0, n_out, out_dtype, residual=None, tn):
    m, ka = a.shape
    _, k_w, n_w = w.shape
    k_len = ka if k_len is None else k_len
    tn = min(tn, n_out)
    while col0 % tn != 0:
        tn //= 2
    assert n_out % tn == 0 and tn % LANES == 0 and k0 % k_len == 0 and k0 < k_w and col0 < n_w
    kblk, joff = k0 // k_len, col0 // tn
    last_col = (n_w - 1) // tn
    in_specs = [pl.BlockSpec((m, k_len), lambda j: (0, kblk)),
                pl.BlockSpec((None, k_len, tn), lambda j: (layer, kblk, jnp.minimum(j + joff, last_col)))]
    args = [a, w]
    if residual is not None:
        in_specs.append(pl.BlockSpec((m, tn), lambda j: (0, j)))
        args.append(residual)
    return pl.pallas_call(
        functools.partial(_mm_cast_body, has_res=residual is not None,
                          cast_kw=_edge_kw(tn, col0, n_out, n_w, k0, k_len, k_w)),
        grid=(n_out // tn,),
        in_specs=in_specs,
        out_specs=(pl.BlockSpec((m, tn), lambda j: (0, j)), pl.BlockSpec((k_len, tn), lambda j: (0, j))),
        out_shape=(jax.ShapeDtypeStruct((m, n_out), out_dtype), jax.ShapeDtypeStruct((k_len, n_out), BF16)),
        compiler_params=_cparams(("arbitrary",)),
        name="matmul_cast",
    )(*args)


def _silu_gate(conv, u):
    return conv * pl.reciprocal(1.0 + jnp.exp2(conv * (-LOG2E))) * u


def _ffn_body(a_ref, wg_ref, wu_ref, cw_ref, cb_ref, act_ref, gtail_ref, tail_ref, *, tm, tn, tiles_per_seq):
    i = pl.program_id(1)

    @pl.when(i % tiles_per_seq == 0)
    def _():
        tail_ref[...] = jnp.zeros_like(tail_ref)

    a = a_ref[...]
    g = jnp.dot(a, wg_ref[...], preferred_element_type=F32)
    u = jnp.dot(a, wu_ref[...], preferred_element_type=F32)
    cw = cw_ref[...]
    prev = tail_ref[...]
    row = lax.broadcasted_iota(jnp.int32, (SUBLANES, tn), 0)
    r1 = pltpu.roll(g, 1, 0)
    r2 = pltpu.roll(g, 2, 0)
    p1 = pltpu.roll(prev, 1, 0)
    p2 = pltpu.roll(prev, 2, 0)
    g1 = jnp.concatenate([jnp.where(row < 1, p1, r1[:SUBLANES]), r1[SUBLANES:]], axis=0)
    g2 = jnp.concatenate([jnp.where(row < 2, p2, r2[:SUBLANES]), r2[SUBLANES:]], axis=0)
    tail = g[tm - SUBLANES:, :]
    gtail_ref[...] = tail
    tail_ref[...] = tail
    conv = cw[0:1, :] * g2 + cw[1:2, :] * g1 + cw[2:3, :] * g + cb_ref[...]
    act_ref[...] = _silu_gate(conv, u).astype(act_ref.dtype)


def _ffn_gate_up(a, w_gate, w_up, conv_w, conv_b, layer, *, seq_len, tm, tn):
    m, ka = a.shape
    _, f = w_gate.shape
    tm, tn = min(tm, m, seq_len), min(tn, f)
    assert m % tm == 0 and f % tn == 0 and seq_len % tm == 0
    tiles_per_seq = seq_len // tm
    w_spec = pl.BlockSpec((ka, tn), lambda j, i: (0, j))
    tile_spec = pl.BlockSpec((tm, tn), lambda j, i: (i, j))
    return pl.pallas_call(
        functools.partial(_ffn_body, tm=tm, tn=tn, tiles_per_seq=tiles_per_seq),
        grid=(f // tn, m // tm),
        in_specs=[pl.BlockSpec((tm, ka), lambda j, i: (i, 0)), w_spec, w_spec,
                  pl.BlockSpec((None, CONV_K, tn), lambda j, i: (layer, 0, j)),
                  pl.BlockSpec((None, 1, tn), lambda j, i: (layer, 0, j))],
        out_specs=(tile_spec, pl.BlockSpec((None, SUBLANES, tn), lambda j, i: (i // tiles_per_seq, 0, j))),
        out_shape=(jax.ShapeDtypeStruct((m, f), BF16), jax.ShapeDtypeStruct((m // seq_len, SUBLANES, f), F32)),
        scratch_shapes=[pltpu.VMEM((SUBLANES, tn), F32)],
        compiler_params=_cparams(("arbitrary", "arbitrary")),
        name="ffn_gate_up",
    )(a, w_gate, w_up, conv_w, conv_b)


def _ffn_sample_body(a_ref, wg_ref, wu_ref, cw_ref, cb_ref, s0_ref, s1_ref,
                     act_ref, g_ref, wgb_ref, wub_ref, *, cast_kw):
    j = pl.program_id(0)
    wg = _cast_weight_block(wg_ref[...], j, **cast_kw)
    wu = _cast_weight_block(wu_ref[...], j, **cast_kw)
    wgb_ref[...] = wg
    wub_ref[...] = wu
    a = a_ref[...]
    g = jnp.dot(a, wg, preferred_element_type=F32)
    u = jnp.dot(a, wu, preferred_element_type=F32)
    g_ref[...] = g
    cw = cw_ref[...]
    conv = cw[0:1, :] * s0_ref[...] + cw[1:2, :] * s1_ref[...] + cw[2:3, :] * g + cb_ref[...]
    act_ref[...] = _silu_gate(conv, u).astype(act_ref.dtype)


def _ffn_sample(a, w_gate, w_up, conv_w, conv_b, layer, *, state, f_pad, tn):
    m, ka = a.shape
    _, _, f = w_gate.shape
    tn = min(tn, f_pad)
    assert f_pad % tn == 0
    last_col = (f - 1) // tn
    w_spec = pl.BlockSpec((None, ka, tn), lambda j: (layer, 0, jnp.minimum(j, last_col)))
    tile_spec = pl.BlockSpec((m, tn), lambda j: (0, j))
    wb_spec = pl.BlockSpec((ka, tn), lambda j: (0, j))
    return pl.pallas_call(
        functools.partial(_ffn_sample_body, cast_kw=_edge_kw(tn, 0, f_pad, f, 0, ka, ka)),
        grid=(f_pad // tn,),
        in_specs=[pl.BlockSpec((m, ka), lambda j: (0, 0)), w_spec, w_spec,
                  pl.BlockSpec((None, CONV_K, tn), lambda j: (layer, 0, j)),
                  pl.BlockSpec((None, 1, tn), lambda j: (layer, 0, j)), tile_spec, tile_spec],
        out_specs=(tile_spec, tile_spec, wb_spec, wb_spec),
        out_shape=(jax.ShapeDtypeStruct((m, f_pad), BF16), jax.ShapeDtypeStruct((m, f_pad), F32),
                   jax.ShapeDtypeStruct((ka, f_pad), BF16), jax.ShapeDtypeStruct((ka, f_pad), BF16)),
        compiler_params=_cparams(("arbitrary",)),
        name="ffn_sample",
    )(a, w_gate, w_up, conv_w, conv_b, *state)


def _neg_abs(x):
    bits = lax.bitcast_convert_type(x, jnp.uint32) | jnp.uint32(0x80000000)
    return lax.bitcast_convert_type(bits, F32)


def _softplus2(z2):
    return jnp.maximum(z2, 0.0) + jnp.log2(1.0 + jnp.exp2(_neg_abs(z2)))


def _split_bf16(x):
    hi = x.astype(BF16)
    lo = (x - hi.astype(F32)).astype(BF16)
    return hi, lo


def _attn_prompt_body(bias_ref, q_ref, k_ref, v_ref, tri_ref, o_ref, *, tq, heads):
    hb, qi = pl.program_id(1), pl.program_id(2)
    tri = tri_ref[...]
    lanes = [slice(c * HEAD_DIM, (c + 1) * HEAD_DIM) for c in range(heads)]
    qs = [q_ref[:, lanes[c]] for c in range(heads)]
    biases = [bias_ref[hb * heads + c] * LOG2E for c in range(heads)]

    def blocks(kb, state, diag):
        ks = pl.multiple_of(kb * tq, tq)
        hs = range(heads)
        s = [lax.dot_general(qs[c], k_ref[pl.ds(ks, tq), lanes[c]].astype(BF16), (((1,), (1,)), ((), ())),
                             preferred_element_type=F32) for c in hs]
        z2 = [s[c] * SCALE2 + biases[c] for c in hs]
        sp = [_softplus2(z2[c]) for c in hs]
        log_beta = [z2[c] - sp[c] for c in hs]
        if diag:
            row = lax.broadcasted_iota(jnp.int32, (tq, tq), 0)
            col = lax.broadcasted_iota(jnp.int32, (tq, tq), 1)
            valid = col < row
            sp = [jnp.where(valid, sp[c], 0.0) for c in hs]
        split = [_split_bf16(sp[c]) for c in hs]
        after = [jnp.dot(jnp.concatenate(split[c], axis=1), tri, preferred_element_type=F32) for c in hs]
        w = [jnp.exp2(log_beta[c] - after[c] - state[c][0]) for c in hs]
        if diag:
            w = [jnp.where(valid, w[c], 0.0) for c in hs]
        pv = [jnp.dot(w[c].astype(BF16), v_ref[pl.ds(ks, tq), lanes[c]].astype(BF16), preferred_element_type=F32)
              for c in hs]
        return tuple((state[c][0] + (after[c][:, 0:1] + sp[c][:, 0:1]), state[c][1] + pv[c]) for c in hs)

    zero = (jnp.zeros((tq, 1), F32), jnp.zeros((tq, HEAD_DIM), F32))
    state = blocks(qi, (zero,) * heads, True)
    state = lax.fori_loop(0, qi, lambda it, st: blocks(qi - 1 - it, st, False), state)
    for c in range(heads):
        o_ref[:, lanes[c]] = state[c][1].astype(o_ref.dtype)


def _suffix_matrix(t):
    j = lax.broadcasted_iota(jnp.int32, (t, t), 0)
    s = lax.broadcasted_iota(jnp.int32, (t, t), 1)
    tri = (j > s).astype(BF16)
    return jnp.concatenate([tri, tri], axis=0)


def _attn_prompt(q, k, v, sb_bias, layer, *, n_seq, seq_len, tq=256, heads=4):
    m, width = q.shape
    n_heads = width // HEAD_DIM
    tq = min(tq, seq_len)
    nq = seq_len // tq
    assert n_heads % heads == 0
    hw = heads * HEAD_DIM
    return pl.pallas_call(
        functools.partial(_attn_prompt_body, tq=tq, heads=heads),
        grid=(n_seq, n_heads // heads, nq),
        in_specs=[
            pl.BlockSpec(memory_space=pltpu.SMEM),
            pl.BlockSpec((tq, hw), lambda b, h, qi: (b * nq + qi, h)),
            pl.BlockSpec((None, seq_len, hw), lambda b, h, qi: (layer, b, h)),
            pl.BlockSpec((None, seq_len, hw), lambda b, h, qi: (layer, b, h)),
            pl.BlockSpec((2 * tq, tq), lambda b, h, qi: (0, 0)),
        ],
        out_specs=pl.BlockSpec((tq, hw), lambda b, h, qi: (b * nq + qi, h)),
        out_shape=jax.ShapeDtypeStruct((m, width), BF16),
        compiler_params=_cparams(("arbitrary", "arbitrary", "arbitrary")),
        name="attn_prompt",
    )(sb_bias[layer], q, k, v, _suffix_matrix(tq))


def _attn_sample_body(pt_ref, q_ref, bias_ref, *refs, n_heads, pages_per_step, page):
    k_refs = refs[:pages_per_step]
    v_refs = refs[pages_per_step:2 * pages_per_step]
    within_ref, total_ref, upper_ref, o_ref, carry_ref, acc_ref = refs[2 * pages_per_step:]
    c = pl.program_id(1)
    rows = page * n_heads
    tiles = rows // LANES

    @pl.when(c == 0)
    def _():
        carry_ref[...] = jnp.zeros_like(carry_ref)
        acc_ref[...] = jnp.zeros_like(acc_ref)

    q = q_ref[...]
    head_row = lax.broadcasted_iota(jnp.int32, (n_heads, LANES), 0)
    own = head_row == lax.broadcasted_iota(jnp.int32, (n_heads, LANES), 1) % n_heads
    fold_row = lax.broadcasted_iota(jnp.int32, (tiles, LANES), 0)
    gs = range(pages_per_step)
    cross = [lax.dot_general(q, k_refs[g][...].astype(BF16), (((1,), (1,)), ((), ())),
                             preferred_element_type=F32) for g in gs]
    z, sp, within, later, total = [], [], [], [], []
    for g in gs:
        s = jnp.zeros((tiles, LANES), F32)
        for o in range(tiles):
            col_sum = jnp.sum(jnp.where(own, cross[g][:, o * LANES:(o + 1) * LANES], 0.0), axis=0, keepdims=True)
            s = jnp.where(fold_row == o, col_sum, s)
        z.append(s * SCALE2 + bias_ref[...])
        sp.append(_softplus2(z[g]))
    for g in gs:
        sp_split = jnp.concatenate(_split_bf16(sp[g]), axis=1)
        within.append(jnp.dot(sp_split, within_ref[...], preferred_element_type=F32))
        total.append(jnp.dot(sp_split, total_ref[...], preferred_element_type=F32))
    for g in gs:
        tot_split = jnp.concatenate(_split_bf16(total[g]), axis=1)
        tot_split = jnp.concatenate([tot_split, jnp.zeros((LANES - tiles, 2 * LANES), BF16)], axis=0)
        both = jnp.dot(upper_ref[...], tot_split, preferred_element_type=F32)
        later.append(both[:, :LANES] + both[:, LANES:])
    carry = carry_ref[...]
    acc = acc_ref[...]
    spread = []
    for g in gs:
        w = jnp.exp2(z[g] - sp[g] - within[g] - later[g] - carry)
        carry = carry + (later[g][0:1, :] + total[g][0:1, :])
        spread.append(jnp.concatenate(
            [jnp.where(own, jnp.broadcast_to(w[o:o + 1, :], (n_heads, LANES)), 0.0) for o in range(tiles)],
            axis=1).astype(BF16))
    for g in gs:
        acc = acc + jnp.dot(spread[g], v_refs[g][...].astype(BF16), preferred_element_type=F32)
    carry_ref[...] = carry
    acc_ref[...] = acc

    @pl.when(c == pl.num_programs(1) - 1)
    def _():
        o_ref[...] = acc.astype(o_ref.dtype)


def _attn_sample(q, cache_k, cache_v, page_table, sb_bias, layer, pages_per_step=8):
    n_seq, width = q.shape
    n_heads = width // HEAD_DIM
    n_pages = page_table.shape[1]
    rows = cache_k.shape[2]
    page = rows // n_heads
    pages_per_step = min(pages_per_step, n_pages)
    tiles = rows // LANES
    keys_per_tile = LANES // n_heads
    assert n_pages % pages_per_step == 0 and LANES % n_heads == 0 and rows % LANES == 0 and tiles % SUBLANES == 0
    bias = jnp.tile(sb_bias[layer] * LOG2E, keys_per_tile)[None, :]
    lane_key = jnp.arange(LANES, dtype=jnp.int32) // n_heads
    lane_head = jnp.arange(LANES, dtype=jnp.int32) % n_heads
    same_head = lane_head[:, None] == lane_head[None, :]
    within = (same_head & (lane_key[:, None] > lane_key[None, :])).astype(BF16)
    total = same_head.astype(BF16)
    upper = jnp.arange(tiles)[None, :] > jnp.arange(tiles)[:, None]
    upper = jnp.pad(upper.astype(BF16), ((0, 0), (0, LANES - tiles)))
    pad_rows = lambda m: jnp.concatenate([m, m], axis=0)

    def kv_spec(g):
        return pl.BlockSpec((None, None, rows, HEAD_DIM),
                            lambda b, c, pt: (layer, pt[b, n_pages - 1 - (c * pages_per_step + g)], 0, 0))

    kv_specs = [kv_spec(g) for g in range(pages_per_step)]
    const = lambda shape: pl.BlockSpec(shape, lambda b, c, pt: (0,) * len(shape))
    out = pl.pallas_call(
        functools.partial(_attn_sample_body, n_heads=n_heads, pages_per_step=pages_per_step, page=page),
        grid_spec=pltpu.PrefetchScalarGridSpec(
            num_scalar_prefetch=1,
            grid=(n_seq, n_pages // pages_per_step),
            in_specs=[pl.BlockSpec((None, n_heads, HEAD_DIM), lambda b, c, pt: (b, 0, 0)),
                      const((1, LANES)), *kv_specs, *kv_specs,
                      const((2 * LANES, LANES)), const((2 * LANES, LANES)), const((tiles, LANES))],
            out_specs=pl.BlockSpec((None, n_heads, HEAD_DIM), lambda b, c, pt: (b, 0, 0)),
            scratch_shapes=[pltpu.VMEM((1, LANES), F32), pltpu.VMEM((n_heads, HEAD_DIM), F32)],
        ),
        out_shape=jax.ShapeDtypeStruct((n_seq, n_heads, HEAD_DIM), BF16),
        compiler_params=_cparams(("arbitrary", "arbitrary")),
        name="attn_sample",
    )(page_table, q.reshape(n_seq, n_heads, HEAD_DIM).astype(BF16), bias,
      *([cache_k] * pages_per_step), *([cache_v] * pages_per_step), pad_rows(within), pad_rows(total), upper)
    return out.reshape(n_seq, width)


def _pool_mix(window_sums, u, counts, wp_ref, scale):
    group = u.shape[1] // len(POOL_WINDOWS)
    outs = []
    for g in range(len(POOL_WINDOWS)):
        cols = slice(g * group, (g + 1) * group)
        diff = window_sums[g] / counts[g] - u[:, cols]
        outs.append(jnp.dot(diff.astype(BF16), wp_ref[g].astype(BF16), preferred_element_type=F32))
    return jnp.concatenate(outs, axis=1) * scale


def _mixer_prompt_body(gb_ref, gc_ref, hc_ref, u_ref, cw_ref, wp_ref, ps_ref,
                       o_ref, ctail_ref, utail_ref, cs_ref, us_ref, *, tm, past_len):
    t = pl.program_id(1)
    chead, uhead = SUBLANES, 2 * SUBLANES
    cwidth = gc_ref.shape[1]

    @pl.when(t == 0)
    def _():
        cs_ref[:chead, :] = jnp.zeros((chead, cs_ref.shape[1]), F32)
        us_ref[:uhead, :] = jnp.zeros((uhead, us_ref.shape[1]), F32)

    c = gc_ref[...] * hc_ref[...]
    cs_ref[chead:, :] = c
    cw = cw_ref[...]
    conv = (cw[0:1, :] * cs_ref[chead - 2:chead - 2 + tm, :] + cw[1:2, :] * cs_ref[chead - 1:chead - 1 + tm, :]
            + cw[2:3, :] * c)
    o_ref[:, :cwidth] = (gb_ref[...] * conv).astype(o_ref.dtype)

    u = u_ref[...]
    us_ref[uhead:, :] = u
    group = u.shape[1] // len(POOL_WINDOWS)
    pos = past_len + t * tm + lax.broadcasted_iota(jnp.int32, (tm, 1), 0)
    sums, counts = [], []
    for g, win in enumerate(POOL_WINDOWS):
        cols = slice(g * group, (g + 1) * group)
        total = u[:, cols]
        for back in range(1, win):
            total = total + us_ref[uhead - back:uhead - back + tm, cols]
        sums.append(total)
        counts.append(jnp.minimum(win, pos + 1).astype(F32))
    o_ref[:, cwidth:] = _pool_mix(sums, u, counts, wp_ref, ps_ref[...]).astype(o_ref.dtype)

    ctail = cs_ref[tm:tm + chead, :]
    utail = us_ref[tm:tm + uhead, :]
    ctail_ref[...] = ctail
    utail_ref[...] = utail
    cs_ref[:chead, :] = ctail
    us_ref[:uhead, :] = utail


def _mixer_prompt(rest, conv_w, w_pool, pool_scale, layer, *, n_seq, seq_len, tm=256):
    m, w4 = rest.shape
    cw = w4 // 4
    tm = min(tm, seq_len)
    nt = seq_len // tm
    n_groups = len(POOL_WINDOWS)
    group = cw // n_groups

    def col_spec(c):
        return pl.BlockSpec((tm, cw), lambda b, t: (b * nt + t, c))

    return pl.pallas_call(
        functools.partial(_mixer_prompt_body, tm=tm, past_len=0),
        grid=(n_seq, nt),
        in_specs=[col_spec(0), col_spec(1), col_spec(2), col_spec(3),
                  pl.BlockSpec((None, CONV_K, cw), lambda b, t: (layer, 0, 0)),
                  pl.BlockSpec((None, n_groups, group, group), lambda b, t: (layer, 0, 0, 0)),
                  pl.BlockSpec((None, 1, cw), lambda b, t: (layer, 0, 0))],
        out_specs=(pl.BlockSpec((tm, 2 * cw), lambda b, t: (b * nt + t, 0)),
                   pl.BlockSpec((None, SUBLANES, cw), lambda b, t: (b, 0, 0)),
                   pl.BlockSpec((None, 2 * SUBLANES, cw), lambda b, t: (b, 0, 0))),
        out_shape=(jax.ShapeDtypeStruct((m, 2 * cw), BF16),
                   jax.ShapeDtypeStruct((n_seq, SUBLANES, cw), F32),
                   jax.ShapeDtypeStruct((n_seq, 2 * SUBLANES, cw), F32)),
        scratch_shapes=[pltpu.VMEM((SUBLANES + tm, cw), F32), pltpu.VMEM((2 * SUBLANES + tm, cw), F32)],
        compiler_params=_cparams(("arbitrary", "arbitrary")),
        name="mixer_prompt",
    )(rest, rest, rest, rest, conv_w, w_pool, pool_scale)


def _mixer_sample_body(rest_ref, sc_ref, sp_ref, cw_ref, wp_ref, ps_ref, o_ref, c_ref, *, past_len):
    cwidth = cw_ref.shape[1]
    gb = rest_ref[:, 0 * cwidth:1 * cwidth]
    c = rest_ref[:, 1 * cwidth:2 * cwidth] * rest_ref[:, 2 * cwidth:3 * cwidth]
    u = rest_ref[:, 3 * cwidth:4 * cwidth]
    cw = cw_ref[...]
    conv = cw[0:1, :] * sc_ref[0] + cw[1:2, :] * sc_ref[1] + cw[2:3, :] * c
    c_ref[...] = c
    o_ref[:, :cwidth] = (gb * conv).astype(o_ref.dtype)

    group = cwidth // len(POOL_WINDOWS)
    sums, counts = [], []
    for g, win in enumerate(POOL_WINDOWS):
        cols = slice(g * group, (g + 1) * group)
        total = u[:, cols]
        for back in range(1, win):
            total = total + sp_ref[POOL_PAD - back][:, cols]
        sums.append(total)
        counts.append(float(min(win, past_len + 1)))
    o_ref[:, cwidth:] = _pool_mix(sums, u, counts, wp_ref, ps_ref[...]).astype(o_ref.dtype)


def _mixer_sample(rest, conv_state, pool_state, conv_w, w_pool, pool_scale, layer, *, past_len):
    n_seq, w4 = rest.shape
    cw = w4 // 4
    n_groups = len(POOL_WINDOWS)
    group = cw // n_groups
    return pl.pallas_call(
        functools.partial(_mixer_sample_body, past_len=past_len),
        grid=(1,),
        in_specs=[pl.BlockSpec((n_seq, w4), lambda i: (0, 0)),
                  pl.BlockSpec((CONV_K - 1, n_seq, cw), lambda i: (0, 0, 0)),
                  pl.BlockSpec((POOL_PAD, n_seq, cw), lambda i: (0, 0, 0)),
                  pl.BlockSpec((None, CONV_K, cw), lambda i: (layer, 0, 0)),
                  pl.BlockSpec((None, n_groups, group, group), lambda i: (layer, 0, 0, 0)),
                  pl.BlockSpec((None, 1, cw), lambda i: (layer, 0, 0))],
        out_specs=(pl.BlockSpec((n_seq, 2 * cw), lambda i: (0, 0)),
                   pl.BlockSpec((n_seq, cw), lambda i: (0, 0))),
        out_shape=(jax.ShapeDtypeStruct((n_seq, 2 * cw), BF16),
                   jax.ShapeDtypeStruct((n_seq, cw), F32)),
        compiler_params=_cparams(("arbitrary",)),
        name="mixer_sample",
    )(rest, conv_state, pool_state, conv_w, w_pool, pool_scale)


def _sample_layer(x, layer, p, sample):
    d = x.shape[1]
    att_w, cw = d // 2, d // 4
    f, f_pad = p['f'], p['f_pad']
    wts = {}

    xn = _rms_norm(x, p['norm_mix'], layer, BF16)
    proj = functools.partial(_matmul_cast, xn, p['w_in'], layer, out_dtype=F32, tn=512)
    q, wts['q'] = proj(col0=0, n_out=att_w)
    k, wts['k'] = proj(col0=att_w, n_out=att_w)
    v, wts['v'] = proj(col0=2 * att_w, n_out=att_w)
    rest, wts['rest'] = proj(col0=3 * att_w, n_out=4 * cw)

    att = _attn_sample(q, sample['cache_k'], sample['cache_v'], sample['page_table'], p['sb_bias'], layer)
    conv_prev, pool_prev = sample['state_conv'][layer], sample['state_pool'][layer]
    mix, c_new = _mixer_sample(rest, conv_prev.transpose(1, 0, 2), pool_prev.transpose(1, 0, 2),
                               p['conv_w'], p['w_pool'], p['pool_scale'], layer, past_len=sample['past_len'])
    conv_state = jnp.concatenate([conv_prev[:, 1:], c_new[:, None]], axis=1)
    pool_state = jnp.concatenate([pool_prev[:, 1:], rest[:, None, 3 * cw:]], axis=1)

    mixed = jnp.concatenate([att, mix], axis=1)
    h, wts['out'] = _matmul_cast(mixed, p['w_out'], layer, col0=0, n_out=d, out_dtype=F32, residual=x, tn=512)

    hn = _rms_norm(h, p['norm_ffn'], layer, BF16)
    ffn_prev = jnp.pad(sample['state_ffn'][layer], ((0, 0), (0, 0), (0, f_pad - f)))
    act, g_new, wts['gate'], wts['up'] = _ffn_sample(
        hn, p['w_ffn_gate'], p['w_ffn_up'], p['ffn_conv_w'], p['ffn_conv_b'], layer,
        state=(ffn_prev[:, 0], ffn_prev[:, 1]), f_pad=f_pad, tn=256)
    ffn_state = jnp.concatenate([ffn_prev[:, 1:], g_new[:, None]], axis=1)[:, :, :f]
    half = f_pad // 2
    down = functools.partial(_matmul_cast, act, p['w_ffn_down'], layer, k_len=half, col0=0, n_out=d,
                             out_dtype=F32, tn=512)
    part, wts['down0'] = down(k0=0, residual=h)
    y, wts['down1'] = down(k0=half, residual=part)
    return (y, k, v, conv_state, pool_state, ffn_state), wts


def _prompt_layer(x, layer, p, wts, kv_stack, *, n_seq, seq_len):
    d = x.shape[1]
    att_w, cw = d // 2, d // 4
    depth, f, half = p['depth'], p['f'], p['f_pad'] // 2
    tiles, res_tiles = dict(tm=1024, tn=1024), dict(tm=512, tn=1024)

    def mm(a, w, **kw):
        return _matmul(a, w[None], 0, col0=0, n_out=w.shape[1], **kw)

    xn = _rms_norm(x, p['norm_mix'], layer, BF16)
    q = mm(xn, wts['q'], out_dtype=BF16, **tiles)
    k = mm(xn, wts['k'], out_dtype=F32, stack=(depth, layer, kv_stack[0]), **tiles)
    v = mm(xn, wts['v'], out_dtype=F32, stack=(depth, layer, kv_stack[1]), **tiles)
    rest = mm(xn, wts['rest'], out_dtype=F32, **tiles)

    att = _attn_prompt(q, k, v, p['sb_bias'], layer, n_seq=n_seq, seq_len=seq_len)
    mix, ctail, utail = _mixer_prompt(rest, p['conv_w'], p['w_pool'], p['pool_scale'], layer,
                                      n_seq=n_seq, seq_len=seq_len)
    conv_state = ctail[:, SUBLANES - (CONV_K - 1):]
    pool_state = utail[:, 2 * SUBLANES - POOL_PAD:]

    h = mm((att, mix), wts['out'], out_dtype=F32, residual=x, **res_tiles)

    hn = _rms_norm(h, p['norm_ffn'], layer, BF16)
    act, gtail = _ffn_gate_up(hn, wts['gate'], wts['up'], p['ffn_conv_w'], p['ffn_conv_b'], layer,
                              seq_len=seq_len, tm=1024, tn=512)
    ffn_state = gtail[:, SUBLANES - (CONV_K - 1):, :f]
    part = _matmul(act, wts['down0'][None], 0, k0=0, k_len=half, col0=0, n_out=d, out_dtype=F32, residual=h,
                   **res_tiles)
    y = _matmul(act, wts['down1'][None], 0, k0=half, w_k0=0, k_len=half, col0=0, n_out=d, out_dtype=F32,
                residual=part, **res_tiles)
    return y, (k, v), conv_state, pool_state, ffn_state


def kernel(x_prompt, x_sample, cache_k, cache_v, state_conv, state_pool, state_ffn, page_table, norm_mix, w_in,
           sb_bias, conv_w, w_pool, pool_scale, w_out, norm_ffn, w_ffn_gate, ffn_conv_w, ffn_conv_b, w_ffn_up,
           w_ffn_down, norm_final):
    bp, seq_len, d = x_prompt.shape
    bs, dec_seq, _ = x_sample.shape
    assert dec_seq == 1
    depth, n_phys, page, n_heads, head_dim = cache_k.shape
    assert head_dim == HEAD_DIM
    past_len = page_table.shape[1] * page
    f = w_ffn_gate.shape[2]
    f_pad = _cdiv(f, FFN_PAD) * FFN_PAD
    pad_f = ((0, 0), (0, 0), (0, f_pad - f))
    p = dict(
        depth=depth, f=f, f_pad=f_pad, norm_mix=norm_mix[:, None, :], sb_bias=sb_bias, conv_w=conv_w,
        w_pool=w_pool, pool_scale=pool_scale[:, None, :], norm_ffn=norm_ffn[:, None, :],
        ffn_conv_w=jnp.pad(ffn_conv_w, pad_f), ffn_conv_b=jnp.pad(ffn_conv_b[:, None, :], pad_f),
        w_in=w_in, w_out=w_out, w_ffn_gate=w_ffn_gate, w_ffn_up=w_ffn_up, w_ffn_down=w_ffn_down)
    sample = dict(
        cache_k=cache_k.reshape(depth, n_phys, page * n_heads, head_dim),
        cache_v=cache_v.reshape(depth, n_phys, page * n_heads, head_dim),
        page_table=page_table, state_conv=state_conv, state_pool=state_pool, state_ffn=state_ffn,
        past_len=past_len)

    yp = x_prompt.reshape(bp * seq_len, d)
    ys = x_sample.reshape(bs, d)
    kv_stack = (None, None)
    outs_p, outs_s = [], []
    for layer in range(depth):
        (ys, *rest_s), wts = _sample_layer(ys, layer, p, sample)
        yp, kv_stack, *rest_p = _prompt_layer(yp, layer, p, wts, kv_stack, n_seq=bp, seq_len=seq_len)
        outs_p.append(rest_p)
        outs_s.append(rest_s)
    final_gain = norm_final[None, None, :]
    yp = _rms_norm(yp, final_gain, 0, F32).reshape(bp, seq_len, d)
    ys = _rms_norm(ys, final_gain, 0, F32).reshape(bs, 1, d)

    def stack(outs, idx, shape):
        return jnp.stack([o[idx] for o in outs]).reshape(shape)

    kv_p = (depth, bp, seq_len, n_heads, head_dim)
    kv_s = (depth, bs, 1, n_heads, head_dim)
    return (yp, ys,
            kv_stack[0].reshape(kv_p), kv_stack[1].reshape(kv_p),
            stack(outs_p, 0, (depth, bp, CONV_K - 1, d // 4)),
            stack(outs_p, 1, (depth, bp, POOL_PAD, d // 4)),
            stack(outs_p, 2, (depth, bp, CONV_K - 1, -1)),
            stack(outs_s, 0, kv_s), stack(outs_s, 1, kv_s),
            stack(outs_s, 2, (depth, bs, CONV_K - 1, d // 4)),
            stack(outs_s, 3, (depth, bs, POOL_PAD, d // 4)),
            stack(outs_s, 4, (depth, bs, CONV_K - 1, -1)))
```

```python
import functools

import jax
import jax.numpy as jnp
from jax import lax
from jax.experimental import pallas as pl
from jax.experimental.pallas import tpu as pltpu

HEAD_DIM = 128
CONV_K = 3
POOL_WINDOWS = (2, 4, 8, 16)
POOL_PAD = max(POOL_WINDOWS) - 1
RMS_EPS = 1e-6
LOG2E = 1.4426950408889634
SCALE2 = HEAD_DIM ** -0.5 * LOG2E

F32 = jnp.float32
BF16 = jnp.bfloat16

VMEM_LIMIT_BYTES = 56 * 1024 * 1024
SUBLANES = 8
LANES = 128

FFN_PAD = 1024


def _cparams(semantics):
    return pltpu.CompilerParams(dimension_semantics=semantics, vmem_limit_bytes=VMEM_LIMIT_BYTES)


def _cdiv(a, b):
    return (a + b - 1) // b


def _norm_body(x_ref, g_ref, o_ref):
    x = x_ref[...]
    ms = jnp.mean(x * x, axis=-1, keepdims=True)
    o_ref[...] = (x * lax.rsqrt(ms + RMS_EPS) * g_ref[...]).astype(o_ref.dtype)


def _rms_norm(x, gains, layer, out_dtype, tm=512):
    m, d = x.shape
    tm = min(tm, m)
    return pl.pallas_call(
        _norm_body,
        grid=(m // tm,),
        in_specs=[pl.BlockSpec((tm, d), lambda i: (i, 0)),
                  pl.BlockSpec((None, 1, d), lambda i: (layer, 0, 0))],
        out_specs=pl.BlockSpec((tm, d), lambda i: (i, 0)),
        out_shape=jax.ShapeDtypeStruct((m, d), out_dtype),
        compiler_params=_cparams(("arbitrary",)),
        name="rmsnorm",
    )(x, gains)


def _mm_body(*refs, n_a, has_res):
    a_refs, (w_ref, *rest) = refs[:n_a], refs[n_a:]
    r_ref = rest[0] if has_res else None
    o_ref = rest[-1]
    acc, k0 = None, 0
    for a_ref in a_refs:
        k1 = k0 + a_ref.shape[1]
        part = jnp.dot(a_ref[...], w_ref[k0:k1, :], preferred_element_type=F32)
        acc, k0 = part if acc is None else acc + part, k1
    if has_res:
        acc = acc + r_ref[...]
    o_ref[...] = acc.astype(o_ref.dtype)


def _matmul(a, w, layer, *, k0=0, w_k0=None, k_len=None, col0, n_out, out_dtype, residual=None, stack=None,
            tm, tn):
    a_list = list(a) if isinstance(a, (tuple, list)) else [a]
    m = a_list[0].shape[0]
    ka = sum(x.shape[1] for x in a_list)
    k_len = ka if k_len is None else k_len
    assert len(a_list) == 1 or k_len == ka
    tm, tn = min(tm, m), min(tn, n_out)
    while col0 % tn != 0:
        tn //= 2
    assert m % tm == 0 and n_out % tn == 0 and tn % LANES == 0 and k0 % k_len == 0
    w_k0 = k0 if w_k0 is None else w_k0
    assert w_k0 % k_len == 0
    kblk, w_kblk, joff = k0 // k_len, w_k0 // k_len, col0 // tn
    if len(a_list) == 1:
        in_specs = [pl.BlockSpec((tm, k_len), lambda j, i: (i, kblk))]
    else:
        in_specs = [pl.BlockSpec((tm, x.shape[1]), lambda j, i: (i, 0)) for x in a_list]
    in_specs.append(pl.BlockSpec((None, k_len, tn), lambda j, i: (layer, w_kblk, j + joff)))
    args = a_list + [w]
    if residual is not None:
        in_specs.append(pl.BlockSpec((tm, tn), lambda j, i: (i, j)))
        args.append(residual)
    aliases = {}
    if stack is None:
        out_spec = pl.BlockSpec((tm, tn), lambda j, i: (i, j))
        out_shape = jax.ShapeDtypeStruct((m, n_out), out_dtype)
    else:
        depth, slot, prev = stack
        out_spec = pl.BlockSpec((None, tm, tn), lambda j, i: (slot, i, j))
        out_shape = jax.ShapeDtypeStruct((depth, m, n_out), out_dtype)
        if prev is not None:
            in_specs.append(pl.BlockSpec(memory_space=pl.ANY))
            args.append(prev)
            aliases = {len(args) - 1: 0}
    body = functools.partial(_mm_body, n_a=len(a_list), has_res=residual is not None)
    if aliases:
        inner = body
        body = lambda *refs: inner(*refs[:-2], refs[-1])
    return pl.pallas_call(
        body,
        grid=(n_out // tn, m // tm),
        in_specs=in_specs,
        out_specs=out_spec,
        out_shape=out_shape,
        input_output_aliases=aliases,
        compiler_params=_cparams(("arbitrary", "arbitrary")),
        name="matmul",
    )(*args)


def _cast_weight_block(w, j, *, tn, n_valid, k_valid):
    if n_valid is not None:
        cols = j * tn + lax.broadcasted_iota(jnp.int32, w.shape, 1)
        w = jnp.where(cols < n_valid, w, 0.0)
    if k_valid is not None:
        rows = lax.broadcasted_iota(jnp.int32, w.shape, 0)
        w = jnp.where(rows < k_valid, w, 0.0)
    return w.astype(BF16)


def _mm_cast_body(*refs, has_res, cast_kw):
    if has_res:
        a_ref, w_ref, r_ref, o_ref, wb_ref = refs
    else:
        a_ref, w_ref, o_ref, wb_ref = refs
    wb = _cast_weight_block(w_ref[...], pl.program_id(0), **cast_kw)
    wb_ref[...] = wb
    acc = jnp.dot(a_ref[...], wb, preferred_element_type=F32)
    if has_res:
        acc = acc + r_ref[...]
    o_ref[...] = acc.astype(o_ref.dtype)


def _edge_kw(tn, col0, n_out, n_w, k0, k_len, k_w):
    return dict(tn=tn, n_valid=(n_w - col0) if col0 + n_out > n_w else None,
                k_valid=(k_w - k0) if k0 + k_len > k_w else None)


def _matmul_cast(a, w, layer, *, k0=0, k_len=None, col0, n_out, out_dtype, residual=None, tn):
    m, ka = a.shape
    _, k_w, n_w = w.shape
    k_len = ka if k_len is None else k_len
    tn = min(tn, n_out)
    while col0 % tn != 0:
        tn //= 2
    assert n_out % tn == 0 and tn % LANES == 0 and k0 % k_len == 0 and k0 < k_w and col0 < n_w
    kblk, joff = k0 // k_len, col0 // tn
    last_col = (n_w - 1) // tn
    in_specs = [pl.BlockSpec((m, k_len), lambda j: (0, kblk)),
                pl.BlockSpec((None, k_len, tn), lambda j: (layer, kblk, jnp.minimum(j + joff, last_col)))]
    args = [a, w]
    if residual is not None:
        in_specs.append(pl.BlockSpec((m, tn), lambda j: (0, j)))
        args.append(residual)
    return pl.pallas_call(
        functools.partial(_mm_cast_body, has_res=residual is not None,
                          cast_kw=_edge_kw(tn, col0, n_out, n_w, k0, k_len, k_w)),
        grid=(n_out // tn,),
        in_specs=in_specs,
        out_specs=(pl.BlockSpec((m, tn), lambda j: (0, j)), pl.BlockSpec((k_len, tn), lambda j: (0, j))),
        out_shape=(jax.ShapeDtypeStruct((m, n_out), out_dtype), jax.ShapeDtypeStruct((k_len, n_out), BF16)),
        compiler_params=_cparams(("arbitrary",)),
        name="matmul_cast",
    )(*args)


def _silu_gate(conv, u):
    return conv * pl.reciprocal(1.0 + jnp.exp2(conv * (-LOG2E))) * u


def _ffn_body(a_ref, wg_ref, wu_ref, cw_ref, cb_ref, act_ref, gtail_ref, tail_ref, *, tm, tn, tiles_per_seq):
    i = pl.program_id(1)

    @pl.when(i % tiles_per_seq == 0)
    def _():
        tail_ref[...] = jnp.zeros_like(tail_ref)

    a = a_ref[...]
    g = jnp.dot(a, wg_ref[...], preferred_element_type=F32)
    u = jnp.dot(a, wu_ref[...], preferred_element_type=F32)
    cw = cw_ref[...]
    prev = tail_ref[...]
    row = lax.broadcasted_iota(jnp.int32, (SUBLANES, tn), 0)
    r1 = pltpu.roll(g, 1, 0)
    r2 = pltpu.roll(g, 2, 0)
    p1 = pltpu.roll(prev, 1, 0)
    p2 = pltpu.roll(prev, 2, 0)
    g1 = jnp.concatenate([jnp.where(row < 1, p1, r1[:SUBLANES]), r1[SUBLANES:]], axis=0)
    g2 = jnp.concatenate([jnp.where(row < 2, p2, r2[:SUBLANES]), r2[SUBLANES:]], axis=0)
    tail = g[tm - SUBLANES:, :]
    gtail_ref[...] = tail
    tail_ref[...] = tail
    conv = cw[0:1, :] * g2 + cw[1:2, :] * g1 + cw[2:3, :] * g + cb_ref[...]
    act_ref[...] = _silu_gate(conv, u).astype(act_ref.dtype)


def _ffn_gate_up(a, w_gate, w_up, conv_w, conv_b, layer, *, seq_len, tm, tn):
    m, ka = a.shape
    _, f = w_gate.shape
    tm, tn = min(tm, m, seq_len), min(tn, f)
    assert m % tm == 0 and f % tn == 0 and seq_len % tm == 0
    tiles_per_seq = seq_len // tm
    w_spec = pl.BlockSpec((ka, tn), lambda j, i: (0, j))
    tile_spec = pl.BlockSpec((tm, tn), lambda j, i: (i, j))
    return pl.pallas_call(
        functools.partial(_ffn_body, tm=tm, tn=tn, tiles_per_seq=tiles_per_seq),
        grid=(f // tn, m // tm),
        in_specs=[pl.BlockSpec((tm, ka), lambda j, i: (i, 0)), w_spec, w_spec,
                  pl.BlockSpec((None, CONV_K, tn), lambda j, i: (layer, 0, j)),
                  pl.BlockSpec((None, 1, tn), lambda j, i: (layer, 0, j))],
        out_specs=(tile_spec, pl.BlockSpec((None, SUBLANES, tn), lambda j, i: (i // tiles_per_seq, 0, j))),
        out_shape=(jax.ShapeDtypeStruct((m, f), BF16), jax.ShapeDtypeStruct((m // seq_len, SUBLANES, f), F32)),
        scratch_shapes=[pltpu.VMEM((SUBLANES, tn), F32)],
        compiler_params=_cparams(("arbitrary", "arbitrary")),
        name="ffn_gate_up",
    )(a, w_gate, w_up, conv_w, conv_b)


def _ffn_sample_body(a_ref, wg_ref, wu_ref, cw_ref, cb_ref, s0_ref, s1_ref,
                     act_ref, g_ref, wgb_ref, wub_ref, *, cast_kw):
    j = pl.program_id(0)
    wg = _cast_weight_block(wg_ref[...], j, **cast_kw)
    wu = _cast_weight_block(wu_ref[...], j, **cast_kw)
    wgb_ref[...] = wg
    wub_ref[...] = wu
    a = a_ref[...]
    g = jnp.dot(a, wg, preferred_element_type=F32)
    u = jnp.dot(a, wu, preferred_element_type=F32)
    g_ref[...] = g
    cw = cw_ref[...]
    conv = cw[0:1, :] * s0_ref[...] + cw[1:2, :] * s1_ref[...] + cw[2:3, :] * g + cb_ref[...]
    act_ref[...] = _silu_gate(conv, u).astype(act_ref.dtype)


def _ffn_sample(a, w_gate, w_up, conv_w, conv_b, layer, *, state, f_pad, tn):
    m, ka = a.shape
    _, _, f = w_gate.shape
    tn = min(tn, f_pad)
    assert f_pad % tn == 0
    last_col = (f - 1) // tn
    w_spec = pl.BlockSpec((None, ka, tn), lambda j: (layer, 0, jnp.minimum(j, last_col)))
    tile_spec = pl.BlockSpec((m, tn), lambda j: (0, j))
    wb_spec = pl.BlockSpec((ka, tn), lambda j: (0, j))
    return pl.pallas_call(
        functools.partial(_ffn_sample_body, cast_kw=_edge_kw(tn, 0, f_pad, f, 0, ka, ka)),
        grid=(f_pad // tn,),
        in_specs=[pl.BlockSpec((m, ka), lambda j: (0, 0)), w_spec, w_spec,
                  pl.BlockSpec((None, CONV_K, tn), lambda j: (layer, 0, j)),
                  pl.BlockSpec((None, 1, tn), lambda j: (layer, 0, j)), tile_spec, tile_spec],
        out_specs=(tile_spec, tile_spec, wb_spec, wb_spec),
        out_shape=(jax.ShapeDtypeStruct((m, f_pad), BF16), jax.ShapeDtypeStruct((m, f_pad), F32),
                   jax.ShapeDtypeStruct((ka, f_pad), BF16), jax.ShapeDtypeStruct((ka, f_pad), BF16)),
        compiler_params=_cparams(("arbitrary",)),
        name="ffn_sample",
    )(a, w_gate, w_up, conv_w, conv_b, *state)


def _neg_abs(x):
    bits = lax.bitcast_convert_type(x, jnp.uint32) | jnp.uint32(0x80000000)
    return lax.bitcast_convert_type(bits, F32)


def _softplus2(z2):
    return jnp.maximum(z2, 0.0) + jnp.log2(1.0 + jnp.exp2(_neg_abs(z2)))


def _split_bf16(x):
    hi = x.astype(BF16)
    lo = (x - hi.astype(F32)).astype(BF16)
    return hi, lo


def _attn_prompt_body(bias_ref, q_ref, k_ref, v_ref, tri_ref, o_ref, carry_ref, acc_ref, *, tq, heads):
    hb, qi = pl.program_id(1), pl.program_id(2)
    tri = tri_ref[...]
    lanes = [slice(c * HEAD_DIM, (c + 1) * HEAD_DIM) for c in range(heads)]
    qs = [q_ref[:, lanes[c]] for c in range(heads)]
    biases = [bias_ref[hb * heads + c] * LOG2E for c in range(heads)]
    hs = range(heads)

    def blocks(kb, diag):
        ks = pl.multiple_of(kb * tq, tq)
        s = [lax.dot_general(qs[c], k_ref[pl.ds(ks, tq), lanes[c]].astype(BF16), (((1,), (1,)), ((), ())),
                             preferred_element_type=F32) for c in hs]
        z2 = [s[c] * SCALE2 + biases[c] for c in hs]
        sp = [_softplus2(z2[c]) for c in hs]
        log_beta = [z2[c] - sp[c] for c in hs]
        if diag:
            row = lax.broadcasted_iota(jnp.int32, (tq, tq), 0)
            col = lax.broadcasted_iota(jnp.int32, (tq, tq), 1)
            valid = col < row
            sp = [jnp.where(valid, sp[c], 0.0) for c in hs]
        split = [_split_bf16(sp[c]) for c in hs]
        after = [jnp.dot(jnp.concatenate(split[c], axis=1), tri, preferred_element_type=F32) for c in hs]
        if diag:
            w = [jnp.where(valid, jnp.exp2(log_beta[c] - after[c]), 0.0) for c in hs]
        else:
            w = [jnp.exp2(log_beta[c] - after[c] - carry_ref[c]) for c in hs]
        for c in hs:
            pv = jnp.dot(w[c].astype(BF16), v_ref[pl.ds(ks, tq), lanes[c]].astype(BF16), preferred_element_type=F32)
            block_total = after[c][:, 0:1] + sp[c][:, 0:1]
            if diag:
                acc_ref[c] = pv
                carry_ref[c] = block_total
            else:
                acc_ref[c] += pv
                carry_ref[c] += block_total

    blocks(qi, True)

    def body(it, _):
        blocks(qi - 1 - it, False)
        return 0

    lax.fori_loop(0, qi, body, 0)
    for c in hs:
        o_ref[:, lanes[c]] = acc_ref[c].astype(o_ref.dtype)


def _suffix_matrix(t):
    j = lax.broadcasted_iota(jnp.int32, (t, t), 0)
    s = lax.broadcasted_iota(jnp.int32, (t, t), 1)
    tri = (j > s).astype(BF16)
    return jnp.concatenate([tri, tri], axis=0)


def _attn_prompt(q, k, v, sb_bias, layer, *, n_seq, seq_len, tq=256, heads=8):
    m, width = q.shape
    n_heads = width // HEAD_DIM
    tq = min(tq, seq_len)
    nq = seq_len // tq
    heads = min(heads, n_heads)
    assert n_heads % heads == 0
    hw = heads * HEAD_DIM
    return pl.pallas_call(
        functools.partial(_attn_prompt_body, tq=tq, heads=heads),
        grid=(n_seq, n_heads // heads, nq),
        in_specs=[
            pl.BlockSpec(memory_space=pltpu.SMEM),
            pl.BlockSpec((tq, hw), lambda b, h, qi: (b * nq + qi, h)),
            pl.BlockSpec((None, seq_len, hw), lambda b, h, qi: (layer, b, h)),
            pl.BlockSpec((None, seq_len, hw), lambda b, h, qi: (layer, b, h)),
            pl.BlockSpec((2 * tq, tq), lambda b, h, qi: (0, 0)),
        ],
        out_specs=pl.BlockSpec((tq, hw), lambda b, h, qi: (b * nq + qi, h)),
        out_shape=jax.ShapeDtypeStruct((m, width), BF16),
        scratch_shapes=[pltpu.VMEM((heads, tq, 1), F32), pltpu.VMEM((heads, tq, HEAD_DIM), F32)],
        compiler_params=_cparams(("arbitrary", "arbitrary", "arbitrary")),
        name="attn_prompt",
    )(sb_bias[layer], q, k, v, _suffix_matrix(tq))


def _attn_sample_body(pt_ref, q_ref, bias_ref, *refs, n_heads, pages_per_step, page):
    k_refs = refs[:pages_per_step]
    v_refs = refs[pages_per_step:2 * pages_per_step]
    within_ref, total_ref, upper_ref, o_ref, carry_ref, acc_ref = refs[2 * pages_per_step:]
    c = pl.program_id(1)
    rows = page * n_heads
    tiles = rows // LANES

    @pl.when(c == 0)
    def _():
        carry_ref[...] = jnp.zeros_like(carry_ref)
        acc_ref[...] = jnp.zeros_like(acc_ref)

    q = q_ref[...]
    head_row = lax.broadcasted_iota(jnp.int32, (n_heads, LANES), 0)
    own = head_row == lax.broadcasted_iota(jnp.int32, (n_heads, LANES), 1) % n_heads
    fold_row = lax.broadcasted_iota(jnp.int32, (tiles, LANES), 0)
    gs = range(pages_per_step)
    cross = [lax.dot_general(q, k_refs[g][...].astype(BF16), (((1,), (1,)), ((), ())),
                             preferred_element_type=F32) for g in gs]
    z, sp, within, later, total = [], [], [], [], []
    for g in gs:
        s = jnp.zeros((tiles, LANES), F32)
        for o in range(tiles):
            col_sum = jnp.sum(jnp.where(own, cross[g][:, o * LANES:(o + 1) * LANES], 0.0), axis=0, keepdims=True)
            s = jnp.where(fold_row == o, col_sum, s)
        z.append(s * SCALE2 + bias_ref[...])
        sp.append(_softplus2(z[g]))
    for g in gs:
        sp_split = jnp.concatenate(_split_bf16(sp[g]), axis=1)
        within.append(jnp.dot(sp_split, within_ref[...], preferred_element_type=F32))
        total.append(jnp.dot(sp_split, total_ref[...], preferred_element_type=F32))
    for g in gs:
        tot_split = jnp.concatenate(_split_bf16(total[g]), axis=1)
        tot_split = jnp.concatenate([tot_split, jnp.zeros((LANES - tiles, 2 * LANES), BF16)], axis=0)
        both = jnp.dot(upper_ref[...], tot_split, preferred_element_type=F32)
        later.append(both[:, :LANES] + both[:, LANES:])
    carry = carry_ref[...]
    acc = acc_ref[...]
    spread = []
    for g in gs:
        w = jnp.exp2(z[g] - sp[g] - within[g] - later[g] - carry)
        carry = carry + (later[g][0:1, :] + total[g][0:1, :])
        spread.append(jnp.concatenate(
            [jnp.where(own, jnp.broadcast_to(w[o:o + 1, :], (n_heads, LANES)), 0.0) for o in range(tiles)],
            axis=1).astype(BF16))
    for g in gs:
        acc = acc + jnp.dot(spread[g], v_refs[g][...].astype(BF16), preferred_element_type=F32)
    carry_ref[...] = carry
    acc_ref[...] = acc

    @pl.when(c == pl.num_programs(1) - 1)
    def _():
        o_ref[...] = acc.astype(o_ref.dtype)


def _attn_sample(q, cache_k, cache_v, page_table, sb_bias, layer, pages_per_step=8):
    n_seq, width = q.shape
    n_heads = width // HEAD_DIM
    n_pages = page_table.shape[1]
    rows = cache_k.shape[2]
    page = rows // n_heads
    pages_per_step = min(pages_per_step, n_pages)
    tiles = rows // LANES
    keys_per_tile = LANES // n_heads
    assert n_pages % pages_per_step == 0 and LANES % n_heads == 0 and rows % LANES == 0 and tiles % SUBLANES == 0
    bias = jnp.tile(sb_bias[layer] * LOG2E, keys_per_tile)[None, :]
    lane_key = jnp.arange(LANES, dtype=jnp.int32) // n_heads
    lane_head = jnp.arange(LANES, dtype=jnp.int32) % n_heads
    same_head = lane_head[:, None] == lane_head[None, :]
    within = (same_head & (lane_key[:, None] > lane_key[None, :])).astype(BF16)
    total = same_head.astype(BF16)
    upper = jnp.arange(tiles)[None, :] > jnp.arange(tiles)[:, None]
    upper = jnp.pad(upper.astype(BF16), ((0, 0), (0, LANES - tiles)))
    pad_rows = lambda m: jnp.concatenate([m, m], axis=0)

    def kv_spec(g):
        return pl.BlockSpec((None, None, rows, HEAD_DIM),
                            lambda b, c, pt: (layer, pt[b, n_pages - 1 - (c * pages_per_step + g)], 0, 0))

    kv_specs = [kv_spec(g) for g in range(pages_per_step)]
    const = lambda shape: pl.BlockSpec(shape, lambda b, c, pt: (0,) * len(shape))
    out = pl.pallas_call(
        functools.partial(_attn_sample_body, n_heads=n_heads, pages_per_step=pages_per_step, page=page),
        grid_spec=pltpu.PrefetchScalarGridSpec(
            num_scalar_prefetch=1,
            grid=(n_seq, n_pages // pages_per_step),
            in_specs=[pl.BlockSpec((None, n_heads, HEAD_DIM), lambda b, c, pt: (b, 0, 0)),
                      const((1, LANES)), *kv_specs, *kv_specs,
                      const((2 * LANES, LANES)), const((2 * LANES, LANES)), const((tiles, LANES))],
            out_specs=pl.BlockSpec((None, n_heads, HEAD_DIM), lambda b, c, pt: (b, 0, 0)),
            scratch_shapes=[pltpu.VMEM((1, LANES), F32), pltpu.VMEM((n_heads, HEAD_DIM), F32)],
        ),
        out_shape=jax.ShapeDtypeStruct((n_seq, n_heads, HEAD_DIM), BF16),
        compiler_params=_cparams(("arbitrary", "arbitrary")),
        name="attn_sample",
    )(page_table, q.reshape(n_seq, n_heads, HEAD_DIM).astype(BF16), bias,
      *([cache_k] * pages_per_step), *([cache_v] * pages_per_step), pad_rows(within), pad_rows(total), upper)
    return out.reshape(n_seq, width)


def _pool_mix(window_sums, u, counts, wp_ref, scale):
    group = u.shape[1] // len(POOL_WINDOWS)
    outs = []
    for g in range(len(POOL_WINDOWS)):
        cols = slice(g * group, (g + 1) * group)
        diff = window_sums[g] / counts[g] - u[:, cols]
        outs.append(jnp.dot(diff.astype(BF16), wp_ref[g].astype(BF16), preferred_element_type=F32))
    return jnp.concatenate(outs, axis=1) * scale


def _mixer_prompt_body(gb_ref, gc_ref, hc_ref, u_ref, cw_ref, wp_ref, ps_ref,
                       o_ref, ctail_ref, utail_ref, cs_ref, us_ref, *, tm, past_len):
    t = pl.program_id(1)
    chead, uhead = SUBLANES, 2 * SUBLANES
    cwidth = gc_ref.shape[1]

    @pl.when(t == 0)
    def _():
        cs_ref[:chead, :] = jnp.zeros((chead, cs_ref.shape[1]), F32)
        us_ref[:uhead, :] = jnp.zeros((uhead, us_ref.shape[1]), F32)

    c = gc_ref[...] * hc_ref[...]
    cs_ref[chead:, :] = c
    cw = cw_ref[...]
    conv = (cw[0:1, :] * cs_ref[chead - 2:chead - 2 + tm, :] + cw[1:2, :] * cs_ref[chead - 1:chead - 1 + tm, :]
            + cw[2:3, :] * c)
    o_ref[:, :cwidth] = (gb_ref[...] * conv).astype(o_ref.dtype)

    u = u_ref[...]
    us_ref[uhead:, :] = u
    group = u.shape[1] // len(POOL_WINDOWS)
    pos = past_len + t * tm + lax.broadcasted_iota(jnp.int32, (tm, 1), 0)
    sums, counts = [], []
    for g, win in enumerate(POOL_WINDOWS):
        cols = slice(g * group, (g + 1) * group)
        total = u[:, cols]
        for back in range(1, win):
            total = total + us_ref[uhead - back:uhead - back + tm, cols]
        sums.append(total)
        counts.append(jnp.minimum(win, pos + 1).astype(F32))
    o_ref[:, cwidth:] = _pool_mix(sums, u, counts, wp_ref, ps_ref[...]).astype(o_ref.dtype)

    ctail = cs_ref[tm:tm + chead, :]
    utail = us_ref[tm:tm + uhead, :]
    ctail_ref[...] = ctail
    utail_ref[...] = utail
    cs_ref[:chead, :] = ctail
    us_ref[:uhead, :] = utail


def _mixer_prompt(rest, conv_w, w_pool, pool_scale, layer, *, n_seq, seq_len, tm=256):
    m, w4 = rest.shape
    cw = w4 // 4
    tm = min(tm, seq_len)
    nt = seq_len // tm
    n_groups = len(POOL_WINDOWS)
    group = cw // n_groups

    def col_spec(c):
        return pl.BlockSpec((tm, cw), lambda b, t: (b * nt + t, c))

    return pl.pallas_call(
        functools.partial(_mixer_prompt_body, tm=tm, past_len=0),
        grid=(n_seq, nt),
        in_specs=[col_spec(0), col_spec(1), col_spec(2), col_spec(3),
                  pl.BlockSpec((None, CONV_K, cw), lambda b, t: (layer, 0, 0)),
                  pl.BlockSpec((None, n_groups, group, group), lambda b, t: (layer, 0, 0, 0)),
                  pl.BlockSpec((None, 1, cw), lambda b, t: (layer, 0, 0))],
        out_specs=(pl.BlockSpec((tm, 2 * cw), lambda b, t: (b * nt + t, 0)),
                   pl.BlockSpec((None, SUBLANES, cw), lambda b, t: (b, 0, 0)),
                   pl.BlockSpec((None, 2 * SUBLANES, cw), lambda b, t: (b, 0, 0))),
        out_shape=(jax.ShapeDtypeStruct((m, 2 * cw), BF16),
                   jax.ShapeDtypeStruct((n_seq, SUBLANES, cw), F32),
                   jax.ShapeDtypeStruct((n_seq, 2 * SUBLANES, cw), F32)),
        scratch_shapes=[pltpu.VMEM((SUBLANES + tm, cw), F32), pltpu.VMEM((2 * SUBLANES + tm, cw), F32)],
        compiler_params=_cparams(("arbitrary", "arbitrary")),
        name="mixer_prompt",
    )(rest, rest, rest, rest, conv_w, w_pool, pool_scale)


def _mixer_sample_body(rest_ref, sc_ref, sp_ref, cw_ref, wp_ref, ps_ref, o_ref, c_ref, *, past_len):
    cwidth = cw_ref.shape[1]
    gb = rest_ref[:, 0 * cwidth:1 * cwidth]
    c = rest_ref[:, 1 * cwidth:2 * cwidth] * rest_ref[:, 2 * cwidth:3 * cwidth]
    u = rest_ref[:, 3 * cwidth:4 * cwidth]
    cw = cw_ref[...]
    conv = cw[0:1, :] * sc_ref[0] + cw[1:2, :] * sc_ref[1] + cw[2:3, :] * c
    c_ref[...] = c
    o_ref[:, :cwidth] = (gb * conv).astype(o_ref.dtype)

    group = cwidth // len(POOL_WINDOWS)
    sums, counts = [], []
    for g, win in enumerate(POOL_WINDOWS):
        cols = slice(g * group, (g + 1) * group)
        total = u[:, cols]
        for back in range(1, win):
            total = total + sp_ref[POOL_PAD - back][:, cols]
        sums.append(total)
        counts.append(float(min(win, past_len + 1)))
    o_ref[:, cwidth:] = _pool_mix(sums, u, counts, wp_ref, ps_ref[...]).astype(o_ref.dtype)


def _mixer_sample(rest, conv_state, pool_state, conv_w, w_pool, pool_scale, layer, *, past_len):
    n_seq, w4 = rest.shape
    cw = w4 // 4
    n_groups = len(POOL_WINDOWS)
    group = cw // n_groups
    return pl.pallas_call(
        functools.partial(_mixer_sample_body, past_len=past_len),
        grid=(1,),
        in_specs=[pl.BlockSpec((n_seq, w4), lambda i: (0, 0)),
                  pl.BlockSpec((CONV_K - 1, n_seq, cw), lambda i: (0, 0, 0)),
                  pl.BlockSpec((POOL_PAD, n_seq, cw), lambda i: (0, 0, 0)),
                  pl.BlockSpec((None, CONV_K, cw), lambda i: (layer, 0, 0)),
                  pl.BlockSpec((None, n_groups, group, group), lambda i: (layer, 0, 0, 0)),
                  pl.BlockSpec((None, 1, cw), lambda i: (layer, 0, 0))],
        out_specs=(pl.BlockSpec((n_seq, 2 * cw), lambda i: (0, 0)),
                   pl.BlockSpec((n_seq, cw), lambda i: (0, 0))),
        out_shape=(jax.ShapeDtypeStruct((n_seq, 2 * cw), BF16),
                   jax.ShapeDtypeStruct((n_seq, cw), F32)),
        compiler_params=_cparams(("arbitrary",)),
        name="mixer_sample",
    )(rest, conv_state, pool_state, conv_w, w_pool, pool_scale)


def _sample_layer(x, layer, p, sample):
    d = x.shape[1]
    att_w, cw = d // 2, d // 4
    f, f_pad = p['f'], p['f_pad']
    wts = {}

    xn = _rms_norm(x, p['norm_mix'], layer, BF16)
    proj = functools.partial(_matmul_cast, xn, p['w_in'], layer, out_dtype=F32, tn=1024)
    q, wts['q'] = proj(col0=0, n_out=att_w)
    k, wts['k'] = proj(col0=att_w, n_out=att_w)
    v, wts['v'] = proj(col0=2 * att_w, n_out=att_w)
    rest, wts['rest'] = proj(col0=3 * att_w, n_out=4 * cw)

    att = _attn_sample(q, sample['cache_k'], sample['cache_v'], sample['page_table'], p['sb_bias'], layer)
    conv_prev, pool_prev = sample['state_conv'][layer], sample['state_pool'][layer]
    mix, c_new = _mixer_sample(rest, conv_prev.transpose(1, 0, 2), pool_prev.transpose(1, 0, 2),
                               p['conv_w'], p['w_pool'], p['pool_scale'], layer, past_len=sample['past_len'])
    conv_state = jnp.concatenate([conv_prev[:, 1:], c_new[:, None]], axis=1)
    pool_state = jnp.concatenate([pool_prev[:, 1:], rest[:, None, 3 * cw:]], axis=1)

    mixed = jnp.concatenate([att, mix], axis=1)
    h, wts['out'] = _matmul_cast(mixed, p['w_out'], layer, col0=0, n_out=d, out_dtype=F32, residual=x, tn=1024)

    hn = _rms_norm(h, p['norm_ffn'], layer, BF16)
    ffn_prev = jnp.pad(sample['state_ffn'][layer], ((0, 0), (0, 0), (0, f_pad - f)))
    act, g_new, wts['gate'], wts['up'] = _ffn_sample(
        hn, p['w_ffn_gate'], p['w_ffn_up'], p['ffn_conv_w'], p['ffn_conv_b'], layer,
        state=(ffn_prev[:, 0], ffn_prev[:, 1]), f_pad=f_pad, tn=512)
    ffn_state = jnp.concatenate([ffn_prev[:, 1:], g_new[:, None]], axis=1)[:, :, :f]
    half = f_pad // 2
    down = functools.partial(_matmul_cast, act, p['w_ffn_down'], layer, k_len=half, col0=0, n_out=d,
                             out_dtype=F32, tn=512)
    part, wts['down0'] = down(k0=0, residual=h)
    y, wts['down1'] = down(k0=half, residual=part)
    return (y, k, v, conv_state, pool_state, ffn_state), wts


def _prompt_layer(x, layer, p, wts, kv_stack, *, n_seq, seq_len):
    d = x.shape[1]
    att_w, cw = d // 2, d // 4
    depth, f, half = p['depth'], p['f'], p['f_pad'] // 2
    tiles, res_tiles = dict(tm=1024, tn=1024), dict(tm=512, tn=1024)

    def mm(a, w, **kw):
        return _matmul(a, w[None], 0, col0=0, n_out=w.shape[1], **kw)

    xn = _rms_norm(x, p['norm_mix'], layer, BF16)
    q = mm(xn, wts['q'], out_dtype=BF16, **tiles)
    k = mm(xn, wts['k'], out_dtype=F32, stack=(depth, layer, kv_stack[0]), **tiles)
    v = mm(xn, wts['v'], out_dtype=F32, stack=(depth, layer, kv_stack[1]), **tiles)
    rest = mm(xn, wts['rest'], out_dtype=F32, **tiles)

    att = _attn_prompt(q, k, v, p['sb_bias'], layer, n_seq=n_seq, seq_len=seq_len)
    mix, ctail, utail = _mixer_prompt(rest, p['conv_w'], p['w_pool'], p['pool_scale'], layer,
                                      n_seq=n_seq, seq_len=seq_len)
    conv_state = ctail[:, SUBLANES - (CONV_K - 1):]
    pool_state = utail[:, 2 * SUBLANES - POOL_PAD:]

    h = mm((att, mix), wts['out'], out_dtype=F32, residual=x, **res_tiles)

    hn = _rms_norm(h, p['norm_ffn'], layer, BF16)
    act, gtail = _ffn_gate_up(hn, wts['gate'], wts['up'], p['ffn_conv_w'], p['ffn_conv_b'], layer,
                              seq_len=seq_len, tm=1024, tn=512)
    ffn_state = gtail[:, SUBLANES - (CONV_K - 1):, :f]
    part = _matmul(act, wts['down0'][None], 0, k0=0, k_len=half, col0=0, n_out=d, out_dtype=F32, residual=h,
                   **res_tiles)
    y = _matmul(act, wts['down1'][None], 0, k0=half, w_k0=0, k_len=half, col0=0, n_out=d, out_dtype=F32,
                residual=part, **res_tiles)
    return y, (k, v), conv_state, pool_state, ffn_state


def kernel(x_prompt, x_sample, cache_k, cache_v, state_conv, state_pool, state_ffn, page_table, norm_mix, w_in,
           sb_bias, conv_w, w_pool, pool_scale, w_out, norm_ffn, w_ffn_gate, ffn_conv_w, ffn_conv_b, w_ffn_up,
           w_ffn_down, norm_final):
    bp, seq_len, d = x_prompt.shape
    bs, dec_seq, _ = x_sample.shape
    assert dec_seq == 1
    depth, n_phys, page, n_heads, head_dim = cache_k.shape
    assert head_dim == HEAD_DIM
    past_len = page_table.shape[1] * page
    f = w_ffn_gate.shape[2]
    f_pad = _cdiv(f, FFN_PAD) * FFN_PAD
    pad_f = ((0, 0), (0, 0), (0, f_pad - f))
    p = dict(
        depth=depth, f=f, f_pad=f_pad, norm_mix=norm_mix[:, None, :], sb_bias=sb_bias, conv_w=conv_w,
        w_pool=w_pool, pool_scale=pool_scale[:, None, :], norm_ffn=norm_ffn[:, None, :],
        ffn_conv_w=jnp.pad(ffn_conv_w, pad_f), ffn_conv_b=jnp.pad(ffn_conv_b[:, None, :], pad_f),
        w_in=w_in, w_out=w_out, w_ffn_gate=w_ffn_gate, w_ffn_up=w_ffn_up, w_ffn_down=w_ffn_down)
    sample = dict(
        cache_k=cache_k.reshape(depth, n_phys, page * n_heads, head_dim),
        cache_v=cache_v.reshape(depth, n_phys, page * n_heads, head_dim),
        page_table=page_table, state_conv=state_conv, state_pool=state_pool, state_ffn=state_ffn,
        past_len=past_len)

    yp = x_prompt.reshape(bp * seq_len, d)
    ys = x_sample.reshape(bs, d)
    kv_stack = (None, None)
    outs_p, outs_s = [], []
    for layer in range(depth):
        (ys, *rest_s), wts = _sample_layer(ys, layer, p, sample)
        yp, kv_stack, *rest_p = _prompt_layer(yp, layer, p, wts, kv_stack, n_seq=bp, seq_len=seq_len)
        outs_p.append(rest_p)
        outs_s.append(rest_s)
    final_gain = norm_final[None, None, :]
    yp = _rms_norm(yp, final_gain, 0, F32).reshape(bp, seq_len, d)
    ys = _rms_norm(ys, final_gain, 0, F32).reshape(bs, 1, d)

    def stack(outs, idx, shape):
        return jnp.stack([o[idx] for o in outs]).reshape(shape)

    kv_p = (depth, bp, seq_len, n_heads, head_dim)
    kv_s = (depth, bs, 1, n_heads, head_dim)
    return (yp, ys,
            kv_stack[0].reshape(kv_p), kv_stack[1].reshape(kv_p),
            stack(outs_p, 0, (depth, bp, CONV_K - 1, d // 4)),
            stack(outs_p, 1, (depth, bp, POOL_PAD, d // 4)),
            stack(outs_p, 2, (depth, bp, CONV_K - 1, -1)),
            stack(outs_s, 0, kv_s), stack(outs_s, 1, kv_s),
            stack(outs_s, 2, (depth, bs, CONV_K - 1, d // 4)),
            stack(outs_s, 3, (depth, bs, POOL_PAD, d // 4)),
            stack(outs_s, 4, (depth, bs, CONV_K - 1, -1)))
```

```python
import functools

import jax
import jax.numpy as jnp
from jax import lax
from jax.experimental import pallas as pl
from jax.experimental.pallas import tpu as pltpu

HEAD_DIM = 128
CONV_K = 3
POOL_WINDOWS = (2, 4, 8, 16)
POOL_PAD = max(POOL_WINDOWS) - 1
RMS_EPS = 1e-6
LOG2E = 1.4426950408889634
SCALE2 = HEAD_DIM ** -0.5 * LOG2E

F32 = jnp.float32
BF16 = jnp.bfloat16

VMEM_LIMIT_BYTES = 56 * 1024 * 1024
SUBLANES = 8
LANES = 128

FFN_PAD = 1024


def _cparams(semantics):
    return pltpu.CompilerParams(dimension_semantics=semantics, vmem_limit_bytes=VMEM_LIMIT_BYTES)


def _cdiv(a, b):
    return (a + b - 1) // b


def _norm_body(x_ref, g_ref, o_ref):
    x = x_ref[...]
    ms = jnp.mean(x * x, axis=-1, keepdims=True)
    o_ref[...] = (x * lax.rsqrt(ms + RMS_EPS) * g_ref[...]).astype(o_ref.dtype)


def _rms_norm(x, gains, layer, out_dtype, tm=512):
    m, d = x.shape
    tm = min(tm, m)
    return pl.pallas_call(
        _norm_body,
        grid=(m // tm,),
        in_specs=[pl.BlockSpec((tm, d), lambda i: (i, 0)),
                  pl.BlockSpec((None, 1, d), lambda i: (layer, 0, 0))],
        out_specs=pl.BlockSpec((tm, d), lambda i: (i, 0)),
        out_shape=jax.ShapeDtypeStruct((m, d), out_dtype),
        compiler_params=_cparams(("arbitrary",)),
        name="rmsnorm",
    )(x, gains)


def _mm_body(*refs, n_a, has_res):
    a_refs, (w_ref, *rest) = refs[:n_a], refs[n_a:]
    r_ref = rest[0] if has_res else None
    o_ref = rest[-1]
    acc, k0 = None, 0
    for a_ref in a_refs:
        k1 = k0 + a_ref.shape[1]
        part = jnp.dot(a_ref[...], w_ref[k0:k1, :], preferred_element_type=F32)
        acc, k0 = part if acc is None else acc + part, k1
    if has_res:
        acc = acc + r_ref[...]
    o_ref[...] = acc.astype(o_ref.dtype)


def _matmul(a, w, layer, *, k0=0, k_len=None, col0, n_out, out_dtype, residual=None, stack=None, tm, tn):
    a_list = list(a) if isinstance(a, (tuple, list)) else [a]
    m = a_list[0].shape[0]
    ka = sum(x.shape[1] for x in a_list)
    k_len = ka if k_len is None else k_len
    assert len(a_list) == 1 or k_len == ka
    tm, tn = min(tm, m), min(tn, n_out)
    while col0 % tn != 0:
        tn //= 2
    assert m % tm == 0 and n_out % tn == 0 and tn % LANES == 0 and k0 % k_len == 0
    kblk, joff = k0 // k_len, col0 // tn
    if len(a_list) == 1:
        in_specs = [pl.BlockSpec((tm, k_len), lambda j, i: (i, kblk))]
    else:
        in_specs = [pl.BlockSpec((tm, x.shape[1]), lambda j, i: (i, 0)) for x in a_list]
    in_specs.append(pl.BlockSpec((None, k_len, tn), lambda j, i: (layer, kblk, j + joff)))
    args = a_list + [w]
    if residual is not None:
        in_specs.append(pl.BlockSpec((tm, tn), lambda j, i: (i, j)))
        args.append(residual)
    aliases = {}
    if stack is None:
        out_spec = pl.BlockSpec((tm, tn), lambda j, i: (i, j))
        out_shape = jax.ShapeDtypeStruct((m, n_out), out_dtype)
    else:
        depth, slot, prev = stack
        out_spec = pl.BlockSpec((None, tm, tn), lambda j, i: (slot, i, j))
        out_shape = jax.ShapeDtypeStruct((depth, m, n_out), out_dtype)
        if prev is not None:
            in_specs.append(pl.BlockSpec(memory_space=pl.ANY))
            args.append(prev)
            aliases = {len(args) - 1: 0}
    body = functools.partial(_mm_body, n_a=len(a_list), has_res=residual is not None)
    if aliases:
        inner = body
        body = lambda *refs: inner(*refs[:-2], refs[-1])
    return pl.pallas_call(
        body,
        grid=(n_out // tn, m // tm),
        in_specs=in_specs,
        out_specs=out_spec,
        out_shape=out_shape,
        input_output_aliases=aliases,
        compiler_params=_cparams(("arbitrary", "arbitrary")),
        name="matmul",
    )(*args)


def _cast_weight_block(w, j, k, *, tn, k_len, n_valid, k_valid):
    if n_valid is not None:
        cols = j * tn + lax.broadcasted_iota(jnp.int32, w.shape, 1)
        w = jnp.where(cols < n_valid, w, 0.0)
    if k_valid is not None:
        rows = k * k_len + lax.broadcasted_iota(jnp.int32, w.shape, 0)
        w = jnp.where(rows < k_valid, w, 0.0)
    return w.astype(BF16)


def _mm_cast_body(*refs, has_res, nk, cast_kw):
    if has_res:
        a_ref, w_ref, r_ref, o_ref, wb_ref = refs
    else:
        a_ref, w_ref, o_ref, wb_ref = refs
    j, k = pl.program_id(0), pl.program_id(1)
    wb = _cast_weight_block(w_ref[...], j, k, **cast_kw)
    wb_ref[...] = wb
    part = jnp.dot(a_ref[...], wb, preferred_element_type=F32)
    if nk == 1:
        o_ref[...] = part + r_ref[...] if has_res else part
    else:
        @pl.when(k == 0)
        def _():
            o_ref[...] = part + r_ref[...] if has_res else part

        @pl.when(k > 0)
        def _():
            o_ref[...] += part


def _matmul_cast(a, w, layer, *, k_len=None, n_pad=None, residual=None, tn):
    m, ka = a.shape
    _, k_w, n_w = w.shape
    k_len = ka if k_len is None else k_len
    n_pad = n_w if n_pad is None else n_pad
    tn = min(tn, n_pad)
    assert n_pad % tn == 0 and tn % LANES == 0 and ka % k_len == 0 and ka >= k_w and ka - k_w < k_len
    nk = ka // k_len
    last_col = (n_w - 1) // tn
    in_specs = [pl.BlockSpec((m, k_len), lambda j, k: (0, k)),
                pl.BlockSpec((None, k_len, tn), lambda j, k: (layer, k, jnp.minimum(j, last_col)))]
    args = [a, w]
    if residual is not None:
        in_specs.append(pl.BlockSpec((m, tn), lambda j, k: (0, j)))
        args.append(residual)
    cast_kw = dict(tn=tn, k_len=k_len, n_valid=n_w if n_pad > n_w else None, k_valid=k_w if ka > k_w else None)
    return pl.pallas_call(
        functools.partial(_mm_cast_body, has_res=residual is not None, nk=nk, cast_kw=cast_kw),
        grid=(n_pad // tn, nk),
        in_specs=in_specs,
        out_specs=(pl.BlockSpec((m, tn), lambda j, k: (0, j)), pl.BlockSpec((k_len, tn), lambda j, k: (k, j))),
        out_shape=(jax.ShapeDtypeStruct((m, n_pad), F32), jax.ShapeDtypeStruct((ka, n_pad), BF16)),
        compiler_params=_cparams(("arbitrary", "arbitrary")),
        name="matmul_cast",
    )(*args)


def _silu_gate(conv, u):
    return conv * pl.reciprocal(1.0 + jnp.exp2(conv * (-LOG2E))) * u


def _ffn_body(a_ref, wg_ref, wu_ref, cw_ref, cb_ref, act_ref, gtail_ref, tail_ref, *, tm, tn, tiles_per_seq):
    i = pl.program_id(1)

    @pl.when(i % tiles_per_seq == 0)
    def _():
        tail_ref[...] = jnp.zeros_like(tail_ref)

    a = a_ref[...]
    g = jnp.dot(a, wg_ref[...], preferred_element_type=F32)
    u = jnp.dot(a, wu_ref[...], preferred_element_type=F32)
    cw = cw_ref[...]
    prev = tail_ref[...]
    row = lax.broadcasted_iota(jnp.int32, (SUBLANES, tn), 0)
    r1 = pltpu.roll(g, 1, 0)
    r2 = pltpu.roll(g, 2, 0)
    p1 = pltpu.roll(prev, 1, 0)
    p2 = pltpu.roll(prev, 2, 0)
    g1 = jnp.concatenate([jnp.where(row < 1, p1, r1[:SUBLANES]), r1[SUBLANES:]], axis=0)
    g2 = jnp.concatenate([jnp.where(row < 2, p2, r2[:SUBLANES]), r2[SUBLANES:]], axis=0)
    tail = g[tm - SUBLANES:, :]
    gtail_ref[...] = tail
    tail_ref[...] = tail
    conv = cw[0:1, :] * g2 + cw[1:2, :] * g1 + cw[2:3, :] * g + cb_ref[...]
    act_ref[...] = _silu_gate(conv, u).astype(act_ref.dtype)


def _ffn_gate_up(a, w_gate, w_up, conv_w, conv_b, layer, *, seq_len, tm, tn):
    m, ka = a.shape
    _, f = w_gate.shape
    tm, tn = min(tm, m, seq_len), min(tn, f)
    assert m % tm == 0 and f % tn == 0 and seq_len % tm == 0
    tiles_per_seq = seq_len // tm
    w_spec = pl.BlockSpec((ka, tn), lambda j, i: (0, j))
    tile_spec = pl.BlockSpec((tm, tn), lambda j, i: (i, j))
    return pl.pallas_call(
        functools.partial(_ffn_body, tm=tm, tn=tn, tiles_per_seq=tiles_per_seq),
        grid=(f // tn, m // tm),
        in_specs=[pl.BlockSpec((tm, ka), lambda j, i: (i, 0)), w_spec, w_spec,
                  pl.BlockSpec((None, CONV_K, tn), lambda j, i: (layer, 0, j)),
                  pl.BlockSpec((None, 1, tn), lambda j, i: (layer, 0, j))],
        out_specs=(tile_spec, pl.BlockSpec((None, SUBLANES, tn), lambda j, i: (i // tiles_per_seq, 0, j))),
        out_shape=(jax.ShapeDtypeStruct((m, f), BF16), jax.ShapeDtypeStruct((m // seq_len, SUBLANES, f), F32)),
        scratch_shapes=[pltpu.VMEM((SUBLANES, tn), F32)],
        compiler_params=_cparams(("arbitrary", "arbitrary")),
        name="ffn_gate_up",
    )(a, w_gate, w_up, conv_w, conv_b)


def _ffn_sample_body(a_ref, wg_ref, wu_ref, cw_ref, cb_ref, s0_ref, s1_ref,
                     act_ref, g_ref, wgb_ref, wub_ref, *, cast_kw):
    j = pl.program_id(0)
    wg = _cast_weight_block(wg_ref[...], j, 0, **cast_kw)
    wu = _cast_weight_block(wu_ref[...], j, 0, **cast_kw)
    wgb_ref[...] = wg
    wub_ref[...] = wu
    a = a_ref[...]
    g = jnp.dot(a, wg, preferred_element_type=F32)
    u = jnp.dot(a, wu, preferred_element_type=F32)
    g_ref[...] = g
    cw = cw_ref[...]
    conv = cw[0:1, :] * s0_ref[...] + cw[1:2, :] * s1_ref[...] + cw[2:3, :] * g + cb_ref[...]
    act_ref[...] = _silu_gate(conv, u).astype(act_ref.dtype)


def _ffn_sample(a, w_gate, w_up, conv_w, conv_b, layer, *, state, f_pad, tn):
    m, ka = a.shape
    _, _, f = w_gate.shape
    tn = min(tn, f_pad)
    assert f_pad % tn == 0
    last_col = (f - 1) // tn
    w_spec = pl.BlockSpec((None, ka, tn), lambda j: (layer, 0, jnp.minimum(j, last_col)))
    tile_spec = pl.BlockSpec((m, tn), lambda j: (0, j))
    wb_spec = pl.BlockSpec((ka, tn), lambda j: (0, j))
    return pl.pallas_call(
        functools.partial(_ffn_sample_body,
                          cast_kw=dict(tn=tn, k_len=ka, n_valid=f if f_pad > f else None, k_valid=None)),
        grid=(f_pad // tn,),
        in_specs=[pl.BlockSpec((m, ka), lambda j: (0, 0)), w_spec, w_spec,
                  pl.BlockSpec((None, CONV_K, tn), lambda j: (layer, 0, j)),
                  pl.BlockSpec((None, 1, tn), lambda j: (layer, 0, j)), tile_spec, tile_spec],
        out_specs=(tile_spec, tile_spec, wb_spec, wb_spec),
        out_shape=(jax.ShapeDtypeStruct((m, f_pad), BF16), jax.ShapeDtypeStruct((m, f_pad), F32),
                   jax.ShapeDtypeStruct((ka, f_pad), BF16), jax.ShapeDtypeStruct((ka, f_pad), BF16)),
        compiler_params=_cparams(("arbitrary",)),
        name="ffn_sample",
    )(a, w_gate, w_up, conv_w, conv_b, *state)


def _neg_abs(x):
    bits = lax.bitcast_convert_type(x, jnp.uint32) | jnp.uint32(0x80000000)
    return lax.bitcast_convert_type(bits, F32)


def _softplus2(z2):
    return jnp.maximum(z2, 0.0) + jnp.log2(1.0 + jnp.exp2(_neg_abs(z2)))


def _split_bf16(x):
    hi = x.astype(BF16)
    lo = (x - hi.astype(F32)).astype(BF16)
    return hi, lo


def _attn_prompt_body(bias_ref, q_ref, k_ref, v_ref, tri_ref, o_ref, carry_ref, acc_ref, *, tq, heads):
    hb, qi = pl.program_id(1), pl.program_id(2)
    tri = tri_ref[...]
    lanes = [slice(c * HEAD_DIM, (c + 1) * HEAD_DIM) for c in range(heads)]
    qs = [q_ref[:, lanes[c]] for c in range(heads)]
    biases = [bias_ref[hb * heads + c] * LOG2E for c in range(heads)]
    hs = range(heads)

    def blocks(kb, diag):
        ks = pl.multiple_of(kb * tq, tq)
        s = [lax.dot_general(qs[c], k_ref[pl.ds(ks, tq), lanes[c]].astype(BF16), (((1,), (1,)), ((), ())),
                             preferred_element_type=F32) for c in hs]
        z2 = [s[c] * SCALE2 + biases[c] for c in hs]
        sp = [_softplus2(z2[c]) for c in hs]
        log_beta = [z2[c] - sp[c] for c in hs]
        if diag:
            row = lax.broadcasted_iota(jnp.int32, (tq, tq), 0)
            col = lax.broadcasted_iota(jnp.int32, (tq, tq), 1)
            valid = col < row
            sp = [jnp.where(valid, sp[c], 0.0) for c in hs]
        after = [jnp.dot(sp[c].astype(BF16), tri, preferred_element_type=F32) for c in hs]
        if diag:
            w = [jnp.where(valid, jnp.exp2(log_beta[c] - after[c]), 0.0) for c in hs]
        else:
            w = [jnp.exp2(log_beta[c] - after[c] - carry_ref[c]) for c in hs]
        for c in hs:
            pv = jnp.dot(w[c].astype(BF16), v_ref[pl.ds(ks, tq), lanes[c]].astype(BF16), preferred_element_type=F32)
            block_total = after[c][:, 0:1] + sp[c][:, 0:1]
            if diag:
                acc_ref[c] = pv
                carry_ref[c] = block_total
            else:
                acc_ref[c] += pv
                carry_ref[c] += block_total

    blocks(qi, True)

    def body(it, _):
        blocks(qi - 1 - it, False)
        return 0

    lax.fori_loop(0, qi, body, 0)
    for c in hs:
        o_ref[:, lanes[c]] = acc_ref[c].astype(o_ref.dtype)


def _suffix_matrix(t):
    j = lax.broadcasted_iota(jnp.int32, (t, t), 0)
    s = lax.broadcasted_iota(jnp.int32, (t, t), 1)
    return (j > s).astype(BF16)


def _attn_prompt(q, k, v, sb_bias, layer, *, n_seq, seq_len, tq=256, heads=8):
    m, width = q.shape
    n_heads = width // HEAD_DIM
    tq = min(tq, seq_len)
    nq = seq_len // tq
    heads = min(heads, n_heads)
    assert n_heads % heads == 0
    hw = heads * HEAD_DIM
    return pl.pallas_call(
        functools.partial(_attn_prompt_body, tq=tq, heads=heads),
        grid=(n_seq, n_heads // heads, nq),
        in_specs=[
            pl.BlockSpec(memory_space=pltpu.SMEM),
            pl.BlockSpec((tq, hw), lambda b, h, qi: (b * nq + qi, h)),
            pl.BlockSpec((None, seq_len, hw), lambda b, h, qi: (layer, b, h)),
            pl.BlockSpec((None, seq_len, hw), lambda b, h, qi: (layer, b, h)),
            pl.BlockSpec((tq, tq), lambda b, h, qi: (0, 0)),
        ],
        out_specs=pl.BlockSpec((tq, hw), lambda b, h, qi: (b * nq + qi, h)),
        out_shape=jax.ShapeDtypeStruct((m, width), BF16),
        scratch_shapes=[pltpu.VMEM((heads, tq, 1), F32), pltpu.VMEM((heads, tq, HEAD_DIM), F32)],
        compiler_params=_cparams(("arbitrary", "arbitrary", "arbitrary")),
        name="attn_prompt",
    )(sb_bias[layer], q, k, v, _suffix_matrix(tq))


def _attn_sample_body(pt_ref, q_ref, bias_ref, *refs, n_heads, pages_per_step, page):
    k_refs = refs[:pages_per_step]
    v_refs = refs[pages_per_step:2 * pages_per_step]
    within_ref, total_ref, upper_ref, o_ref, carry_ref, acc_ref = refs[2 * pages_per_step:]
    c = pl.program_id(1)
    rows = page * n_heads
    tiles = rows // LANES

    @pl.when(c == 0)
    def _():
        carry_ref[...] = jnp.zeros_like(carry_ref)
        acc_ref[...] = jnp.zeros_like(acc_ref)

    q = q_ref[...]
    head_row = lax.broadcasted_iota(jnp.int32, (n_heads, LANES), 0)
    own = head_row == lax.broadcasted_iota(jnp.int32, (n_heads, LANES), 1) % n_heads
    fold_row = lax.broadcasted_iota(jnp.int32, (tiles, LANES), 0)
    gs = range(pages_per_step)
    cross = [lax.dot_general(q, k_refs[g][...].astype(BF16), (((1,), (1,)), ((), ())),
                             preferred_element_type=F32) for g in gs]
    z, sp, within, later, total = [], [], [], [], []
    for g in gs:
        s = jnp.zeros((tiles, LANES), F32)
        for o in range(tiles):
            col_sum = jnp.sum(jnp.where(own, cross[g][:, o * LANES:(o + 1) * LANES], 0.0), axis=0, keepdims=True)
            s = jnp.where(fold_row == o, col_sum, s)
        z.append(s * SCALE2 + bias_ref[...])
        sp.append(_softplus2(z[g]))
    for g in gs:
        sp_split = jnp.concatenate(_split_bf16(sp[g]), axis=1)
        within.append(jnp.dot(sp_split, within_ref[...], preferred_element_type=F32))
        total.append(jnp.dot(sp_split, total_ref[...], preferred_element_type=F32))
    for g in gs:
        tot_split = jnp.concatenate(_split_bf16(total[g]), axis=1)
        tot_split = jnp.concatenate([tot_split, jnp.zeros((LANES - tiles, 2 * LANES), BF16)], axis=0)
        both = jnp.dot(upper_ref[...], tot_split, preferred_element_type=F32)
        later.append(both[:, :LANES] + both[:, LANES:])
    carry = carry_ref[...]
    acc = acc_ref[...]
    spread = []
    for g in gs:
        w = jnp.exp2(z[g] - sp[g] - within[g] - later[g] - carry)
        carry = carry + (later[g][0:1, :] + total[g][0:1, :])
        spread.append(jnp.concatenate(
            [jnp.where(own, jnp.broadcast_to(w[o:o + 1, :], (n_heads, LANES)), 0.0) for o in range(tiles)],
            axis=1).astype(BF16))
    for g in gs:
        acc = acc + jnp.dot(spread[g], v_refs[g][...].astype(BF16), preferred_element_type=F32)
    carry_ref[...] = carry
    acc_ref[...] = acc

    @pl.when(c == pl.num_programs(1) - 1)
    def _():
        o_ref[...] = acc.astype(o_ref.dtype)


def _attn_sample(q, cache_k, cache_v, page_table, sb_bias, layer, pages_per_step=8):
    n_seq, width = q.shape
    n_heads = width // HEAD_DIM
    n_pages = page_table.shape[1]
    rows = cache_k.shape[2]
    page = rows // n_heads
    pages_per_step = min(pages_per_step, n_pages)
    tiles = rows // LANES
    keys_per_tile = LANES // n_heads
    assert n_pages % pages_per_step == 0 and LANES % n_heads == 0 and rows % LANES == 0 and tiles % SUBLANES == 0
    bias = jnp.tile(sb_bias[layer] * LOG2E, keys_per_tile)[None, :]
    lane_key = jnp.arange(LANES, dtype=jnp.int32) // n_heads
    lane_head = jnp.arange(LANES, dtype=jnp.int32) % n_heads
    same_head = lane_head[:, None] == lane_head[None, :]
    within = (same_head & (lane_key[:, None] > lane_key[None, :])).astype(BF16)
    total = same_head.astype(BF16)
    upper = jnp.arange(tiles)[None, :] > jnp.arange(tiles)[:, None]
    upper = jnp.pad(upper.astype(BF16), ((0, 0), (0, LANES - tiles)))
    pad_rows = lambda m: jnp.concatenate([m, m], axis=0)

    def kv_spec(g):
        return pl.BlockSpec((None, None, rows, HEAD_DIM),
                            lambda b, c, pt: (layer, pt[b, n_pages - 1 - (c * pages_per_step + g)], 0, 0))

    kv_specs = [kv_spec(g) for g in range(pages_per_step)]
    const = lambda shape: pl.BlockSpec(shape, lambda b, c, pt: (0,) * len(shape))
    out = pl.pallas_call(
        functools.partial(_attn_sample_body, n_heads=n_heads, pages_per_step=pages_per_step, page=page),
        grid_spec=pltpu.PrefetchScalarGridSpec(
            num_scalar_prefetch=1,
            grid=(n_seq, n_pages // pages_per_step),
            in_specs=[pl.BlockSpec((None, n_heads, HEAD_DIM), lambda b, c, pt: (b, 0, 0)),
                      const((1, LANES)), *kv_specs, *kv_specs,
                      const((2 * LANES, LANES)), const((2 * LANES, LANES)), const((tiles, LANES))],
            out_specs=pl.BlockSpec((None, n_heads, HEAD_DIM), lambda b, c, pt: (b, 0, 0)),
            scratch_shapes=[pltpu.VMEM((1, LANES), F32), pltpu.VMEM((n_heads, HEAD_DIM), F32)],
        ),
        out_shape=jax.ShapeDtypeStruct((n_seq, n_heads, HEAD_DIM), BF16),
        compiler_params=_cparams(("arbitrary", "arbitrary")),
        name="attn_sample",
    )(page_table, q.reshape(n_seq, n_heads, HEAD_DIM).astype(BF16), bias,
      *([cache_k] * pages_per_step), *([cache_v] * pages_per_step), pad_rows(within), pad_rows(total), upper)
    return out.reshape(n_seq, width)


def _pool_mix(window_sums, u, counts, wp_ref, scale):
    group = u.shape[1] // len(POOL_WINDOWS)
    outs = []
    for g in range(len(POOL_WINDOWS)):
        cols = slice(g * group, (g + 1) * group)
        diff = window_sums[g] / counts[g] - u[:, cols]
        outs.append(jnp.dot(diff.astype(BF16), wp_ref[g].astype(BF16), preferred_element_type=F32))
    return jnp.concatenate(outs, axis=1) * scale


def _mixer_prompt_body(gb_ref, gc_ref, hc_ref, u_ref, cw_ref, wp_ref, ps_ref,
                       o_ref, ctail_ref, utail_ref, cs_ref, us_ref, *, tm, past_len):
    t = pl.program_id(1)
    chead, uhead = SUBLANES, 2 * SUBLANES
    cwidth = gc_ref.shape[1]

    @pl.when(t == 0)
    def _():
        cs_ref[:chead, :] = jnp.zeros((chead, cs_ref.shape[1]), F32)
        us_ref[:uhead, :] = jnp.zeros((uhead, us_ref.shape[1]), F32)

    c = gc_ref[...] * hc_ref[...]
    cs_ref[chead:, :] = c
    cw = cw_ref[...]
    conv = (cw[0:1, :] * cs_ref[chead - 2:chead - 2 + tm, :] + cw[1:2, :] * cs_ref[chead - 1:chead - 1 + tm, :]
            + cw[2:3, :] * c)
    o_ref[:, :cwidth] = (gb_ref[...] * conv).astype(o_ref.dtype)

    u = u_ref[...]
    us_ref[uhead:, :] = u
    group = u.shape[1] // len(POOL_WINDOWS)
    pos = past_len + t * tm + lax.broadcasted_iota(jnp.int32, (tm, 1), 0)
    sums, counts = [], []
    for g, win in enumerate(POOL_WINDOWS):
        cols = slice(g * group, (g + 1) * group)
        total = u[:, cols]
        for back in range(1, win):
            total = total + us_ref[uhead - back:uhead - back + tm, cols]
        sums.append(total)
        counts.append(jnp.minimum(win, pos + 1).astype(F32))
    o_ref[:, cwidth:] = _pool_mix(sums, u, counts, wp_ref, ps_ref[...]).astype(o_ref.dtype)

    ctail = cs_ref[tm:tm + chead, :]
    utail = us_ref[tm:tm + uhead, :]
    ctail_ref[...] = ctail
    utail_ref[...] = utail
    cs_ref[:chead, :] = ctail
    us_ref[:uhead, :] = utail


def _mixer_prompt(rest, conv_w, w_pool, pool_scale, layer, *, n_seq, seq_len, tm=256):
    m, w4 = rest.shape
    cw = w4 // 4
    tm = min(tm, seq_len)
    nt = seq_len // tm
    n_groups = len(POOL_WINDOWS)
    group = cw // n_groups

    def col_spec(c):
        return pl.BlockSpec((tm, cw), lambda b, t: (b * nt + t, c))

    return pl.pallas_call(
        functools.partial(_mixer_prompt_body, tm=tm, past_len=0),
        grid=(n_seq, nt),
        in_specs=[col_spec(0), col_spec(1), col_spec(2), col_spec(3),
                  pl.BlockSpec((None, CONV_K, cw), lambda b, t: (layer, 0, 0)),
                  pl.BlockSpec((None, n_groups, group, group), lambda b, t: (layer, 0, 0, 0)),
                  pl.BlockSpec((None, 1, cw), lambda b, t: (layer, 0, 0))],
        out_specs=(pl.BlockSpec((tm, 2 * cw), lambda b, t: (b * nt + t, 0)),
                   pl.BlockSpec((None, SUBLANES, cw), lambda b, t: (b, 0, 0)),
                   pl.BlockSpec((None, 2 * SUBLANES, cw), lambda b, t: (b, 0, 0))),
        out_shape=(jax.ShapeDtypeStruct((m, 2 * cw), BF16),
                   jax.ShapeDtypeStruct((n_seq, SUBLANES, cw), F32),
                   jax.ShapeDtypeStruct((n_seq, 2 * SUBLANES, cw), F32)),
        scratch_shapes=[pltpu.VMEM((SUBLANES + tm, cw), F32), pltpu.VMEM((2 * SUBLANES + tm, cw), F32)],
        compiler_params=_cparams(("arbitrary", "arbitrary")),
        name="mixer_prompt",
    )(rest, rest, rest, rest, conv_w, w_pool, pool_scale)


def _mixer_sample_body(rest_ref, sc_ref, sp_ref, cw_ref, wp_ref, ps_ref, o_ref, c_ref, *, past_len):
    cwidth = cw_ref.shape[1]
    gb = rest_ref[:, 0 * cwidth:1 * cwidth]
    c = rest_ref[:, 1 * cwidth:2 * cwidth] * rest_ref[:, 2 * cwidth:3 * cwidth]
    u = rest_ref[:, 3 * cwidth:4 * cwidth]
    cw = cw_ref[...]
    conv = cw[0:1, :] * sc_ref[0] + cw[1:2, :] * sc_ref[1] + cw[2:3, :] * c
    c_ref[...] = c
    o_ref[:, :cwidth] = (gb * conv).astype(o_ref.dtype)

    group = cwidth // len(POOL_WINDOWS)
    sums, counts = [], []
    for g, win in enumerate(POOL_WINDOWS):
        cols = slice(g * group, (g + 1) * group)
        total = u[:, cols]
        for back in range(1, win):
            total = total + sp_ref[POOL_PAD - back][:, cols]
        sums.append(total)
        counts.append(float(min(win, past_len + 1)))
    o_ref[:, cwidth:] = _pool_mix(sums, u, counts, wp_ref, ps_ref[...]).astype(o_ref.dtype)


def _mixer_sample(rest, conv_state, pool_state, conv_w, w_pool, pool_scale, layer, *, past_len):
    n_seq, w4 = rest.shape
    cw = w4 // 4
    n_groups = len(POOL_WINDOWS)
    group = cw // n_groups
    return pl.pallas_call(
        functools.partial(_mixer_sample_body, past_len=past_len),
        grid=(1,),
        in_specs=[pl.BlockSpec((n_seq, w4), lambda i: (0, 0)),
                  pl.BlockSpec((CONV_K - 1, n_seq, cw), lambda i: (0, 0, 0)),
                  pl.BlockSpec((POOL_PAD, n_seq, cw), lambda i: (0, 0, 0)),
                  pl.BlockSpec((None, CONV_K, cw), lambda i: (layer, 0, 0)),
                  pl.BlockSpec((None, n_groups, group, group), lambda i: (layer, 0, 0, 0)),
                  pl.BlockSpec((None, 1, cw), lambda i: (layer, 0, 0))],
        out_specs=(pl.BlockSpec((n_seq, 2 * cw), lambda i: (0, 0)),
                   pl.BlockSpec((n_seq, cw), lambda i: (0, 0))),
        out_shape=(jax.ShapeDtypeStruct((n_seq, 2 * cw), BF16),
                   jax.ShapeDtypeStruct((n_seq, cw), F32)),
        compiler_params=_cparams(("arbitrary",)),
        name="mixer_sample",
    )(rest, conv_state, pool_state, conv_w, w_pool, pool_scale)


def _sample_layer(x, layer, p, sample):
    d = x.shape[1]
    att_w, cw = d // 2, d // 4
    f, f_pad = p['f'], p['f_pad']
    wts = {}

    xn = _rms_norm(x, p['norm_mix'], layer, BF16)
    proj, wts['in'] = _matmul_cast(xn, p['w_in'], layer, tn=1024)
    q, k, v, rest = (proj[:, :att_w], proj[:, att_w:2 * att_w], proj[:, 2 * att_w:3 * att_w], proj[:, 3 * att_w:])

    att = _attn_sample(q, sample['cache_k'], sample['cache_v'], sample['page_table'], p['sb_bias'], layer)
    conv_prev, pool_prev = sample['state_conv'][layer], sample['state_pool'][layer]
    mix, c_new = _mixer_sample(rest, conv_prev.transpose(1, 0, 2), pool_prev.transpose(1, 0, 2),
                               p['conv_w'], p['w_pool'], p['pool_scale'], layer, past_len=sample['past_len'])
    conv_state = jnp.concatenate([conv_prev[:, 1:], c_new[:, None]], axis=1)
    pool_state = jnp.concatenate([pool_prev[:, 1:], rest[:, None, 3 * cw:]], axis=1)

    mixed = jnp.concatenate([att, mix], axis=1)
    h, wts['out'] = _matmul_cast(mixed, p['w_out'], layer, residual=x, tn=1024)

    hn = _rms_norm(h, p['norm_ffn'], layer, BF16)
    ffn_prev = jnp.pad(sample['state_ffn'][layer], ((0, 0), (0, 0), (0, f_pad - f)))
    act, g_new, wts['gate'], wts['up'] = _ffn_sample(
        hn, p['w_ffn_gate'], p['w_ffn_up'], p['ffn_conv_w'], p['ffn_conv_b'], layer,
        state=(ffn_prev[:, 0], ffn_prev[:, 1]), f_pad=f_pad, tn=512)
    ffn_state = jnp.concatenate([ffn_prev[:, 1:], g_new[:, None]], axis=1)[:, :, :f]
    y, wts['down'] = _matmul_cast(act, p['w_ffn_down'], layer, k_len=f_pad // 2, residual=h, tn=512)
    return (y, k, v, conv_state, pool_state, ffn_state), wts


def _prompt_layer(x, layer, p, wts, kv_stack, *, n_seq, seq_len):
    d = x.shape[1]
    att_w, cw = d // 2, d // 4
    depth, f, half = p['depth'], p['f'], p['f_pad'] // 2
    tiles, res_tiles = dict(tm=1024, tn=1024), dict(tm=512, tn=1024)

    xn = _rms_norm(x, p['norm_mix'], layer, BF16)
    proj = functools.partial(_matmul, xn, wts['in'][None], 0, **tiles)
    q = proj(col0=0, n_out=att_w, out_dtype=BF16)
    k = proj(col0=att_w, n_out=att_w, out_dtype=F32, stack=(depth, layer, kv_stack[0]))
    v = proj(col0=2 * att_w, n_out=att_w, out_dtype=F32, stack=(depth, layer, kv_stack[1]))
    rest = proj(col0=3 * att_w, n_out=4 * cw, out_dtype=F32)

    att = _attn_prompt(q, k, v, p['sb_bias'], layer, n_seq=n_seq, seq_len=seq_len)
    mix, ctail, utail = _mixer_prompt(rest, p['conv_w'], p['w_pool'], p['pool_scale'], layer,
                                      n_seq=n_seq, seq_len=seq_len)
    conv_state = ctail[:, SUBLANES - (CONV_K - 1):]
    pool_state = utail[:, 2 * SUBLANES - POOL_PAD:]

    h = _matmul((att, mix), wts['out'][None], 0, col0=0, n_out=d, out_dtype=F32, residual=x, **res_tiles)

    hn = _rms_norm(h, p['norm_ffn'], layer, BF16)
    act, gtail = _ffn_gate_up(hn, wts['gate'], wts['up'], p['ffn_conv_w'], p['ffn_conv_b'], layer,
                              seq_len=seq_len, tm=1024, tn=512)
    ffn_state = gtail[:, SUBLANES - (CONV_K - 1):, :f]
    down = functools.partial(_matmul, act, wts['down'][None], 0, k_len=half, col0=0, n_out=d, out_dtype=F32,
                             **res_tiles)
    y = down(k0=half, residual=down(k0=0, residual=h))
    return y, (k, v), conv_state, pool_state, ffn_state


def kernel(x_prompt, x_sample, cache_k, cache_v, state_conv, state_pool, state_ffn, page_table, norm_mix, w_in,
           sb_bias, conv_w, w_pool, pool_scale, w_out, norm_ffn, w_ffn_gate, ffn_conv_w, ffn_conv_b, w_ffn_up,
           w_ffn_down, norm_final):
    bp, seq_len, d = x_prompt.shape
    bs, dec_seq, _ = x_sample.shape
    assert dec_seq == 1
    depth, n_phys, page, n_heads, head_dim = cache_k.shape
    assert head_dim == HEAD_DIM
    past_len = page_table.shape[1] * page
    f = w_ffn_gate.shape[2]
    f_pad = _cdiv(f, FFN_PAD) * FFN_PAD
    pad_f = ((0, 0), (0, 0), (0, f_pad - f))
    p = dict(
        depth=depth, f=f, f_pad=f_pad, norm_mix=norm_mix[:, None, :], sb_bias=sb_bias, conv_w=conv_w,
        w_pool=w_pool, pool_scale=pool_scale[:, None, :], norm_ffn=norm_ffn[:, None, :],
        ffn_conv_w=jnp.pad(ffn_conv_w, pad_f), ffn_conv_b=jnp.pad(ffn_conv_b[:, None, :], pad_f),
        w_in=w_in, w_out=w_out, w_ffn_gate=w_ffn_gate, w_ffn_up=w_ffn_up, w_ffn_down=w_ffn_down)
    sample = dict(
        cache_k=cache_k.reshape(depth, n_phys, page * n_heads, head_dim),
        cache_v=cache_v.reshape(depth, n_phys, page * n_heads, head_dim),
        page_table=page_table, state_conv=state_conv, state_pool=state_pool, state_ffn=state_ffn,
        past_len=past_len)

    yp = x_prompt.reshape(bp * seq_len, d)
    ys = x_sample.reshape(bs, d)
    kv_stack = (None, None)
    outs_p, outs_s = [], []
    for layer in range(depth):
        (ys, *rest_s), wts = _sample_layer(ys, layer, p, sample)
        yp, kv_stack, *rest_p = _prompt_layer(yp, layer, p, wts, kv_stack, n_seq=bp, seq_len=seq_len)
        outs_p.append(rest_p)
        outs_s.append(rest_s)
    final_gain = norm_final[None, None, :]
    yp = _rms_norm(yp, final_gain, 0, F32).reshape(bp, seq_len, d)
    ys = _rms_norm(ys, final_gain, 0, F32).reshape(bs, 1, d)

    def stack(outs, idx, shape):
        return jnp.stack([o[idx] for o in outs]).reshape(shape)

    kv_p = (depth, bp, seq_len, n_heads, head_dim)
    kv_s = (depth, bs, 1, n_heads, head_dim)
    return (yp, ys,
            kv_stack[0].reshape(kv_p), kv_stack[1].reshape(kv_p),
            stack(outs_p, 0, (depth, bp, CONV_K - 1, d // 4)),
            stack(outs_p, 1, (depth, bp, POOL_PAD, d // 4)),
            stack(outs_p, 2, (depth, bp, CONV_K - 1, -1)),
            stack(outs_s, 0, kv_s), stack(outs_s, 1, kv_s),
            stack(outs_s, 2, (depth, bs, CONV_K - 1, d // 4)),
            stack(outs_s, 3, (depth, bs, POOL_PAD, d // 4)),
            stack(outs_s, 4, (depth, bs, CONV_K - 1, -1)))
```

```python
import functools

import jax
import jax.numpy as jnp
from jax import lax
from jax.experimental import pallas as pl
from jax.experimental.pallas import tpu as pltpu

HEAD_DIM = 128
CONV_K = 3
POOL_WINDOWS = (2, 4, 8, 16)
POOL_PAD = max(POOL_WINDOWS) - 1
RMS_EPS = 1e-6
LOG2E = 1.4426950408889634
SCALE2 = HEAD_DIM ** -0.5 * LOG2E

F32 = jnp.float32
BF16 = jnp.bfloat16

VMEM_LIMIT_BYTES = 56 * 1024 * 1024
SUBLANES = 8
LANES = 128

FFN_PAD = 1024


def _cparams(semantics):
    return pltpu.CompilerParams(dimension_semantics=semantics, vmem_limit_bytes=VMEM_LIMIT_BYTES)


def _cdiv(a, b):
    return (a + b - 1) // b


def _norm_body(x_ref, g_ref, o_ref):
    x = x_ref[...]
    ms = jnp.mean(x * x, axis=-1, keepdims=True)
    o_ref[...] = (x * lax.rsqrt(ms + RMS_EPS) * g_ref[...]).astype(o_ref.dtype)


def _rms_norm(x, gains, layer, out_dtype, tm=512):
    m, d = x.shape
    tm = min(tm, m)
    return pl.pallas_call(
        _norm_body,
        grid=(m // tm,),
        in_specs=[pl.BlockSpec((tm, d), lambda i: (i, 0)),
                  pl.BlockSpec((None, 1, d), lambda i: (layer, 0, 0))],
        out_specs=pl.BlockSpec((tm, d), lambda i: (i, 0)),
        out_shape=jax.ShapeDtypeStruct((m, d), out_dtype),
        compiler_params=_cparams(("arbitrary",)),
        name="rmsnorm",
    )(x, gains)


def _mm_body(*refs, n_a, has_res):
    a_refs, (w_ref, *rest) = refs[:n_a], refs[n_a:]
    r_ref = rest[0] if has_res else None
    o_ref = rest[-1]
    acc, k0 = None, 0
    for a_ref in a_refs:
        k1 = k0 + a_ref.shape[1]
        part = jnp.dot(a_ref[...], w_ref[k0:k1, :], preferred_element_type=F32)
        acc, k0 = part if acc is None else acc + part, k1
    if has_res:
        acc = acc + r_ref[...]
    o_ref[...] = acc.astype(o_ref.dtype)


def _matmul(a, w, layer, *, k0=0, k_len=None, col0, n_out, out_dtype, residual=None, stack=None, tm, tn):
    a_list = list(a) if isinstance(a, (tuple, list)) else [a]
    m = a_list[0].shape[0]
    ka = sum(x.shape[1] for x in a_list)
    k_len = ka if k_len is None else k_len
    assert len(a_list) == 1 or k_len == ka
    tm, tn = min(tm, m), min(tn, n_out)
    while col0 % tn != 0:
        tn //= 2
    assert m % tm == 0 and n_out % tn == 0 and tn % LANES == 0 and k0 % k_len == 0
    kblk, joff = k0 // k_len, col0 // tn
    if len(a_list) == 1:
        in_specs = [pl.BlockSpec((tm, k_len), lambda j, i: (i, kblk))]
    else:
        in_specs = [pl.BlockSpec((tm, x.shape[1]), lambda j, i: (i, 0)) for x in a_list]
    in_specs.append(pl.BlockSpec((None, k_len, tn), lambda j, i: (layer, kblk, j + joff)))
    args = a_list + [w]
    if residual is not None:
        in_specs.append(pl.BlockSpec((tm, tn), lambda j, i: (i, j)))
        args.append(residual)
    aliases = {}
    if stack is None:
        out_spec = pl.BlockSpec((tm, tn), lambda j, i: (i, j))
        out_shape = jax.ShapeDtypeStruct((m, n_out), out_dtype)
    else:
        depth, slot, prev = stack
        out_spec = pl.BlockSpec((None, tm, tn), lambda j, i: (slot, i, j))
        out_shape = jax.ShapeDtypeStruct((depth, m, n_out), out_dtype)
        if prev is not None:
            in_specs.append(pl.BlockSpec(memory_space=pl.ANY))
            args.append(prev)
            aliases = {len(args) - 1: 0}
    body = functools.partial(_mm_body, n_a=len(a_list), has_res=residual is not None)
    if aliases:
        inner = body
        body = lambda *refs: inner(*refs[:-2], refs[-1])
    return pl.pallas_call(
        body,
        grid=(n_out // tn, m // tm),
        in_specs=in_specs,
        out_specs=out_spec,
        out_shape=out_shape,
        input_output_aliases=aliases,
        compiler_params=_cparams(("arbitrary", "arbitrary")),
        name="matmul",
    )(*args)


def _cast_weight_block(w, j, k, *, tn, k_len, n_valid, k_valid):
    if n_valid is not None:
        cols = j * tn + lax.broadcasted_iota(jnp.int32, w.shape, 1)
        w = jnp.where(cols < n_valid, w, 0.0)
    if k_valid is not None:
        rows = k * k_len + lax.broadcasted_iota(jnp.int32, w.shape, 0)
        w = jnp.where(rows < k_valid, w, 0.0)
    return w.astype(BF16)


def _mm_cast_body(*refs, has_res, nk, cast_kw):
    if has_res:
        a_ref, w_ref, r_ref, o_ref, wb_ref = refs
    else:
        a_ref, w_ref, o_ref, wb_ref = refs
    j, k = pl.program_id(0), pl.program_id(1)
    wb = _cast_weight_block(w_ref[...], j, k, **cast_kw)
    wb_ref[...] = wb
    part = jnp.dot(a_ref[...], wb, preferred_element_type=F32)
    if nk == 1:
        o_ref[...] = part + r_ref[...] if has_res else part
    else:
        @pl.when(k == 0)
        def _():
            o_ref[...] = part + r_ref[...] if has_res else part

        @pl.when(k > 0)
        def _():
            o_ref[...] += part


def _matmul_cast(a, w, layer, *, k_len=None, n_pad=None, residual=None, tn):
    m, ka = a.shape
    _, k_w, n_w = w.shape
    k_len = ka if k_len is None else k_len
    n_pad = n_w if n_pad is None else n_pad
    tn = min(tn, n_pad)
    assert n_pad % tn == 0 and tn % LANES == 0 and ka % k_len == 0 and ka >= k_w and ka - k_w < k_len
    nk = ka // k_len
    last_col = (n_w - 1) // tn
    in_specs = [pl.BlockSpec((m, k_len), lambda j, k: (0, k)),
                pl.BlockSpec((None, k_len, tn), lambda j, k: (layer, k, jnp.minimum(j, last_col)))]
    args = [a, w]
    if residual is not None:
        in_specs.append(pl.BlockSpec((m, tn), lambda j, k: (0, j)))
        args.append(residual)
    cast_kw = dict(tn=tn, k_len=k_len, n_valid=n_w if n_pad > n_w else None, k_valid=k_w if ka > k_w else None)
    return pl.pallas_call(
        functools.partial(_mm_cast_body, has_res=residual is not None, nk=nk, cast_kw=cast_kw),
        grid=(n_pad // tn, nk),
        in_specs=in_specs,
        out_specs=(pl.BlockSpec((m, tn), lambda j, k: (0, j)), pl.BlockSpec((k_len, tn), lambda j, k: (k, j))),
        out_shape=(jax.ShapeDtypeStruct((m, n_pad), F32), jax.ShapeDtypeStruct((ka, n_pad), BF16)),
        compiler_params=_cparams(("arbitrary", "arbitrary")),
        name="matmul_cast",
    )(*args)


def _silu_gate(conv, u):
    return conv * pl.reciprocal(1.0 + jnp.exp2(conv * (-LOG2E))) * u


def _ffn_body(a_ref, wg_ref, wu_ref, cw_ref, cb_ref, act_ref, gtail_ref, tail_ref, *, tm, tn, tiles_per_seq):
    i = pl.program_id(1)

    @pl.when(i % tiles_per_seq == 0)
    def _():
        tail_ref[...] = jnp.zeros_like(tail_ref)

    a = a_ref[...]
    g = jnp.dot(a, wg_ref[...], preferred_element_type=F32)
    u = jnp.dot(a, wu_ref[...], preferred_element_type=F32)
    cw = cw_ref[...]
    prev = tail_ref[...]
    row = lax.broadcasted_iota(jnp.int32, (SUBLANES, tn), 0)
    r1 = pltpu.roll(g, 1, 0)
    r2 = pltpu.roll(g, 2, 0)
    p1 = pltpu.roll(prev, 1, 0)
    p2 = pltpu.roll(prev, 2, 0)
    g1 = jnp.concatenate([jnp.where(row < 1, p1, r1[:SUBLANES]), r1[SUBLANES:]], axis=0)
    g2 = jnp.concatenate([jnp.where(row < 2, p2, r2[:SUBLANES]), r2[SUBLANES:]], axis=0)
    tail = g[tm - SUBLANES:, :]
    gtail_ref[...] = tail
    tail_ref[...] = tail
    conv = cw[0:1, :] * g2 + cw[1:2, :] * g1 + cw[2:3, :] * g + cb_ref[...]
    act_ref[...] = _silu_gate(conv, u).astype(act_ref.dtype)


def _ffn_gate_up(a, w_gate, w_up, conv_w, conv_b, layer, *, seq_len, tm, tn):
    m, ka = a.shape
    _, f = w_gate.shape
    tm, tn = min(tm, m, seq_len), min(tn, f)
    assert m % tm == 0 and f % tn == 0 and seq_len % tm == 0
    tiles_per_seq = seq_len // tm
    w_spec = pl.BlockSpec((ka, tn), lambda j, i: (0, j))
    tile_spec = pl.BlockSpec((tm, tn), lambda j, i: (i, j))
    return pl.pallas_call(
        functools.partial(_ffn_body, tm=tm, tn=tn, tiles_per_seq=tiles_per_seq),
        grid=(f // tn, m // tm),
        in_specs=[pl.BlockSpec((tm, ka), lambda j, i: (i, 0)), w_spec, w_spec,
                  pl.BlockSpec((None, CONV_K, tn), lambda j, i: (layer, 0, j)),
                  pl.BlockSpec((None, 1, tn), lambda j, i: (layer, 0, j))],
        out_specs=(tile_spec, pl.BlockSpec((None, SUBLANES, tn), lambda j, i: (i // tiles_per_seq, 0, j))),
        out_shape=(jax.ShapeDtypeStruct((m, f), BF16), jax.ShapeDtypeStruct((m // seq_len, SUBLANES, f), F32)),
        scratch_shapes=[pltpu.VMEM((SUBLANES, tn), F32)],
        compiler_params=_cparams(("arbitrary", "arbitrary")),
        name="ffn_gate_up",
    )(a, w_gate, w_up, conv_w, conv_b)


def _ffn_sample_body(a_ref, wg_ref, wu_ref, cw_ref, cb_ref, s0_ref, s1_ref,
                     act_ref, g_ref, wgb_ref, wub_ref, *, cast_kw):
    j = pl.program_id(0)
    wg = _cast_weight_block(wg_ref[...], j, 0, **cast_kw)
    wu = _cast_weight_block(wu_ref[...], j, 0, **cast_kw)
    wgb_ref[...] = wg
    wub_ref[...] = wu
    a = a_ref[...]
    g = jnp.dot(a, wg, preferred_element_type=F32)
    u = jnp.dot(a, wu, preferred_element_type=F32)
    g_ref[...] = g
    cw = cw_ref[...]
    conv = cw[0:1, :] * s0_ref[...] + cw[1:2, :] * s1_ref[...] + cw[2:3, :] * g + cb_ref[...]
    act_ref[...] = _silu_gate(conv, u).astype(act_ref.dtype)


def _ffn_sample(a, w_gate, w_up, conv_w, conv_b, layer, *, state, f_pad, tn):
    m, ka = a.shape
    _, _, f = w_gate.shape
    tn = min(tn, f_pad)
    assert f_pad % tn == 0
    last_col = (f - 1) // tn
    w_spec = pl.BlockSpec((None, ka, tn), lambda j: (layer, 0, jnp.minimum(j, last_col)))
    tile_spec = pl.BlockSpec((m, tn), lambda j: (0, j))
    wb_spec = pl.BlockSpec((ka, tn), lambda j: (0, j))
    return pl.pallas_call(
        functools.partial(_ffn_sample_body,
                          cast_kw=dict(tn=tn, k_len=ka, n_valid=f if f_pad > f else None, k_valid=None)),
        grid=(f_pad // tn,),
        in_specs=[pl.BlockSpec((m, ka), lambda j: (0, 0)), w_spec, w_spec,
                  pl.BlockSpec((None, CONV_K, tn), lambda j: (layer, 0, j)),
                  pl.BlockSpec((None, 1, tn), lambda j: (layer, 0, j)), tile_spec, tile_spec],
        out_specs=(tile_spec, tile_spec, wb_spec, wb_spec),
        out_shape=(jax.ShapeDtypeStruct((m, f_pad), BF16), jax.ShapeDtypeStruct((m, f_pad), F32),
                   jax.ShapeDtypeStruct((ka, f_pad), BF16), jax.ShapeDtypeStruct((ka, f_pad), BF16)),
        compiler_params=_cparams(("arbitrary",)),
        name="ffn_sample",
    )(a, w_gate, w_up, conv_w, conv_b, *state)


def _neg_abs(x):
    bits = lax.bitcast_convert_type(x, jnp.uint32) | jnp.uint32(0x80000000)
    return lax.bitcast_convert_type(bits, F32)


def _softplus2(z2):
    return jnp.maximum(z2, 0.0) + jnp.log2(1.0 + jnp.exp2(_neg_abs(z2)))


def _split_bf16(x):
    hi = x.astype(BF16)
    lo = (x - hi.astype(F32)).astype(BF16)
    return hi, lo


def _attn_prompt_body(bias_ref, q_ref, k_ref, v_ref, tri_ref, o_ref, carry_ref, acc_ref, *, tq, heads):
    hb, qi = pl.program_id(1), pl.program_id(2)
    tri = tri_ref[...]
    lanes = [slice(c * HEAD_DIM, (c + 1) * HEAD_DIM) for c in range(heads)]
    qs = [q_ref[:, lanes[c]] for c in range(heads)]
    biases = [bias_ref[hb * heads + c] * LOG2E for c in range(heads)]
    hs = range(heads)

    def blocks(kb, diag):
        ks = pl.multiple_of(kb * tq, tq)
        s = [lax.dot_general(qs[c], k_ref[pl.ds(ks, tq), lanes[c]].astype(BF16), (((1,), (1,)), ((), ())),
                             preferred_element_type=F32) for c in hs]
        z2 = [s[c] * SCALE2 + biases[c] for c in hs]
        sp = [_softplus2(z2[c]) for c in hs]
        log_beta = [z2[c] - sp[c] for c in hs]
        if diag:
            row = lax.broadcasted_iota(jnp.int32, (tq, tq), 0)
            col = lax.broadcasted_iota(jnp.int32, (tq, tq), 1)
            valid = col < row
            sp = [jnp.where(valid, sp[c], 0.0) for c in hs]
        after = [jnp.dot(sp[c].astype(BF16), tri, preferred_element_type=F32) for c in hs]
        if diag:
            w = [jnp.where(valid, jnp.exp2(log_beta[c] - after[c]), 0.0) for c in hs]
        else:
            w = [jnp.exp2(log_beta[c] - after[c] - carry_ref[c]) for c in hs]
        for c in hs:
            pv = jnp.dot(w[c].astype(BF16), v_ref[pl.ds(ks, tq), lanes[c]].astype(BF16), preferred_element_type=F32)
            block_total = after[c][:, 0:1] + sp[c][:, 0:1]
            if diag:
                acc_ref[c] = pv
                carry_ref[c] = block_total
            else:
                acc_ref[c] += pv
                carry_ref[c] += block_total

    blocks(qi, True)

    def body(it, _):
        blocks(qi - 1 - 2 * it, False)
        blocks(qi - 2 - 2 * it, False)
        return 0

    lax.fori_loop(0, qi // 2, body, 0)

    @pl.when(qi % 2 == 1)
    def _():
        blocks(0, False)

    for c in hs:
        o_ref[:, lanes[c]] = acc_ref[c].astype(o_ref.dtype)


def _suffix_matrix(t):
    j = lax.broadcasted_iota(jnp.int32, (t, t), 0)
    s = lax.broadcasted_iota(jnp.int32, (t, t), 1)
    return (j > s).astype(BF16)


def _attn_prompt(q, k, v, sb_bias, layer, *, n_seq, seq_len, tq=256, heads=8):
    m, width = q.shape
    n_heads = width // HEAD_DIM
    tq = min(tq, seq_len)
    nq = seq_len // tq
    heads = min(heads, n_heads)
    assert n_heads % heads == 0
    hw = heads * HEAD_DIM
    return pl.pallas_call(
        functools.partial(_attn_prompt_body, tq=tq, heads=heads),
        grid=(n_seq, n_heads // heads, nq),
        in_specs=[
            pl.BlockSpec(memory_space=pltpu.SMEM),
            pl.BlockSpec((tq, hw), lambda b, h, qi: (b * nq + qi, h)),
            pl.BlockSpec((None, seq_len, hw), lambda b, h, qi: (layer, b, h)),
            pl.BlockSpec((None, seq_len, hw), lambda b, h, qi: (layer, b, h)),
            pl.BlockSpec((tq, tq), lambda b, h, qi: (0, 0)),
        ],
        out_specs=pl.BlockSpec((tq, hw), lambda b, h, qi: (b * nq + qi, h)),
        out_shape=jax.ShapeDtypeStruct((m, width), BF16),
        scratch_shapes=[pltpu.VMEM((heads, tq, 1), F32), pltpu.VMEM((heads, tq, HEAD_DIM), F32)],
        compiler_params=_cparams(("arbitrary", "arbitrary", "arbitrary")),
        name="attn_prompt",
    )(sb_bias[layer], q, k, v, _suffix_matrix(tq))


def _attn_sample_body(pt_ref, q_ref, bias_ref, *refs, n_heads, pages_per_step, page):
    k_refs = refs[:pages_per_step]
    v_refs = refs[pages_per_step:2 * pages_per_step]
    within_ref, total_ref, upper_ref, o_ref, carry_ref, acc_ref = refs[2 * pages_per_step:]
    c = pl.program_id(1)
    rows = page * n_heads
    tiles = rows // LANES

    @pl.when(c == 0)
    def _():
        carry_ref[...] = jnp.zeros_like(carry_ref)
        acc_ref[...] = jnp.zeros_like(acc_ref)

    q = q_ref[...]
    head_row = lax.broadcasted_iota(jnp.int32, (n_heads, LANES), 0)
    own = head_row == lax.broadcasted_iota(jnp.int32, (n_heads, LANES), 1) % n_heads
    fold_row = lax.broadcasted_iota(jnp.int32, (tiles, LANES), 0)
    gs = range(pages_per_step)
    cross = [lax.dot_general(q, k_refs[g][...].astype(BF16), (((1,), (1,)), ((), ())),
                             preferred_element_type=F32) for g in gs]
    z, sp, within, later, total = [], [], [], [], []
    for g in gs:
        s = jnp.zeros((tiles, LANES), F32)
        for o in range(tiles):
            col_sum = jnp.sum(jnp.where(own, cross[g][:, o * LANES:(o + 1) * LANES], 0.0), axis=0, keepdims=True)
            s = jnp.where(fold_row == o, col_sum, s)
        z.append(s * SCALE2 + bias_ref[...])
        sp.append(_softplus2(z[g]))
    for g in gs:
        sp_split = jnp.concatenate(_split_bf16(sp[g]), axis=1)
        within.append(jnp.dot(sp_split, within_ref[...], preferred_element_type=F32))
        total.append(jnp.dot(sp_split, total_ref[...], preferred_element_type=F32))
    for g in gs:
        tot_split = jnp.concatenate(_split_bf16(total[g]), axis=1)
        tot_split = jnp.concatenate([tot_split, jnp.zeros((LANES - tiles, 2 * LANES), BF16)], axis=0)
        both = jnp.dot(upper_ref[...], tot_split, preferred_element_type=F32)
        later.append(both[:, :LANES] + both[:, LANES:])
    carry = carry_ref[...]
    acc = acc_ref[...]
    spread = []
    for g in gs:
        w = jnp.exp2(z[g] - sp[g] - within[g] - later[g] - carry)
        carry = carry + (later[g][0:1, :] + total[g][0:1, :])
        spread.append(jnp.concatenate(
            [jnp.where(own, jnp.broadcast_to(w[o:o + 1, :], (n_heads, LANES)), 0.0) for o in range(tiles)],
            axis=1).astype(BF16))
    for g in gs:
        acc = acc + jnp.dot(spread[g], v_refs[g][...].astype(BF16), preferred_element_type=F32)
    carry_ref[...] = carry
    acc_ref[...] = acc

    @pl.when(c == pl.num_programs(1) - 1)
    def _():
        o_ref[...] = acc.astype(o_ref.dtype)


def _attn_sample(q, cache_k, cache_v, page_table, sb_bias, layer, pages_per_step=8):
    n_seq, width = q.shape
    n_heads = width // HEAD_DIM
    n_pages = page_table.shape[1]
    rows = cache_k.shape[2]
    page = rows // n_heads
    pages_per_step = min(pages_per_step, n_pages)
    tiles = rows // LANES
    keys_per_tile = LANES // n_heads
    assert n_pages % pages_per_step == 0 and LANES % n_heads == 0 and rows % LANES == 0 and tiles % SUBLANES == 0
    bias = jnp.tile(sb_bias[layer] * LOG2E, keys_per_tile)[None, :]
    lane_key = jnp.arange(LANES, dtype=jnp.int32) // n_heads
    lane_head = jnp.arange(LANES, dtype=jnp.int32) % n_heads
    same_head = lane_head[:, None] == lane_head[None, :]
    within = (same_head & (lane_key[:, None] > lane_key[None, :])).astype(BF16)
    total = same_head.astype(BF16)
    upper = jnp.arange(tiles)[None, :] > jnp.arange(tiles)[:, None]
    upper = jnp.pad(upper.astype(BF16), ((0, 0), (0, LANES - tiles)))
    pad_rows = lambda m: jnp.concatenate([m, m], axis=0)

    def kv_spec(g):
        return pl.BlockSpec((None, None, rows, HEAD_DIM),
                            lambda b, c, pt: (layer, pt[b, n_pages - 1 - (c * pages_per_step + g)], 0, 0))

    kv_specs = [kv_spec(g) for g in range(pages_per_step)]
    const = lambda shape: pl.BlockSpec(shape, lambda b, c, pt: (0,) * len(shape))
    out = pl.pallas_call(
        functools.partial(_attn_sample_body, n_heads=n_heads, pages_per_step=pages_per_step, page=page),
        grid_spec=pltpu.PrefetchScalarGridSpec(
            num_scalar_prefetch=1,
            grid=(n_seq, n_pages // pages_per_step),
            in_specs=[pl.BlockSpec((None, n_heads, HEAD_DIM), lambda b, c, pt: (b, 0, 0)),
                      const((1, LANES)), *kv_specs, *kv_specs,
                      const((2 * LANES, LANES)), const((2 * LANES, LANES)), const((tiles, LANES))],
            out_specs=pl.BlockSpec((None, n_heads, HEAD_DIM), lambda b, c, pt: (b, 0, 0)),
            scratch_shapes=[pltpu.VMEM((1, LANES), F32), pltpu.VMEM((n_heads, HEAD_DIM), F32)],
        ),
        out_shape=jax.ShapeDtypeStruct((n_seq, n_heads, HEAD_DIM), BF16),
        compiler_params=_cparams(("arbitrary", "arbitrary")),
        name="attn_sample",
    )(page_table, q.reshape(n_seq, n_heads, HEAD_DIM).astype(BF16), bias,
      *([cache_k] * pages_per_step), *([cache_v] * pages_per_step), pad_rows(within), pad_rows(total), upper)
    return out.reshape(n_seq, width)


def _pool_mix(window_sums, u, counts, wp_ref, scale):
    group = u.shape[1] // len(POOL_WINDOWS)
    outs = []
    for g in range(len(POOL_WINDOWS)):
        cols = slice(g * group, (g + 1) * group)
        diff = window_sums[g] / counts[g] - u[:, cols]
        outs.append(jnp.dot(diff.astype(BF16), wp_ref[g].astype(BF16), preferred_element_type=F32))
    return jnp.concatenate(outs, axis=1) * scale


def _mixer_prompt_body(gb_ref, gc_ref, hc_ref, u_ref, cw_ref, wp_ref, ps_ref,
                       o_ref, ctail_ref, utail_ref, cs_ref, us_ref, *, tm, past_len):
    t = pl.program_id(1)
    chead, uhead = SUBLANES, 2 * SUBLANES
    cwidth = gc_ref.shape[1]

    @pl.when(t == 0)
    def _():
        cs_ref[:chead, :] = jnp.zeros((chead, cs_ref.shape[1]), F32)
        us_ref[:uhead, :] = jnp.zeros((uhead, us_ref.shape[1]), F32)

    c = gc_ref[...] * hc_ref[...]
    cs_ref[chead:, :] = c
    cw = cw_ref[...]
    conv = (cw[0:1, :] * cs_ref[chead - 2:chead - 2 + tm, :] + cw[1:2, :] * cs_ref[chead - 1:chead - 1 + tm, :]
            + cw[2:3, :] * c)
    o_ref[:, :cwidth] = (gb_ref[...] * conv).astype(o_ref.dtype)

    u = u_ref[...]
    us_ref[uhead:, :] = u
    group = u.shape[1] // len(POOL_WINDOWS)
    pos = past_len + t * tm + lax.broadcasted_iota(jnp.int32, (tm, 1), 0)
    sums, counts = [], []
    for g, win in enumerate(POOL_WINDOWS):
        cols = slice(g * group, (g + 1) * group)
        total = u[:, cols]
        for back in range(1, win):
            total = total + us_ref[uhead - back:uhead - back + tm, cols]
        sums.append(total)
        counts.append(jnp.minimum(win, pos + 1).astype(F32))
    o_ref[:, cwidth:] = _pool_mix(sums, u, counts, wp_ref, ps_ref[...]).astype(o_ref.dtype)

    ctail = cs_ref[tm:tm + chead, :]
    utail = us_ref[tm:tm + uhead, :]
    ctail_ref[...] = ctail
    utail_ref[...] = utail
    cs_ref[:chead, :] = ctail
    us_ref[:uhead, :] = utail


def _mixer_prompt(rest, conv_w, w_pool, pool_scale, layer, *, n_seq, seq_len, tm=512):
    m, w4 = rest.shape
    cw = w4 // 4
    tm = min(tm, seq_len)
    nt = seq_len // tm
    n_groups = len(POOL_WINDOWS)
    group = cw // n_groups

    def col_spec(c):
        return pl.BlockSpec((tm, cw), lambda b, t: (b * nt + t, c))

    return pl.pallas_call(
        functools.partial(_mixer_prompt_body, tm=tm, past_len=0),
        grid=(n_seq, nt),
        in_specs=[col_spec(0), col_spec(1), col_spec(2), col_spec(3),
                  pl.BlockSpec((None, CONV_K, cw), lambda b, t: (layer, 0, 0)),
                  pl.BlockSpec((None, n_groups, group, group), lambda b, t: (layer, 0, 0, 0)),
                  pl.BlockSpec((None, 1, cw), lambda b, t: (layer, 0, 0))],
        out_specs=(pl.BlockSpec((tm, 2 * cw), lambda b, t: (b * nt + t, 0)),
                   pl.BlockSpec((None, SUBLANES, cw), lambda b, t: (b, 0, 0)),
                   pl.BlockSpec((None, 2 * SUBLANES, cw), lambda b, t: (b, 0, 0))),
        out_shape=(jax.ShapeDtypeStruct((m, 2 * cw), BF16),
                   jax.ShapeDtypeStruct((n_seq, SUBLANES, cw), F32),
                   jax.ShapeDtypeStruct((n_seq, 2 * SUBLANES, cw), F32)),
        scratch_shapes=[pltpu.VMEM((SUBLANES + tm, cw), F32), pltpu.VMEM((2 * SUBLANES + tm, cw), F32)],
        compiler_params=_cparams(("arbitrary", "arbitrary")),
        name="mixer_prompt",
    )(rest, rest, rest, rest, conv_w, w_pool, pool_scale)


def _mixer_sample_body(rest_ref, sc_ref, sp_ref, cw_ref, wp_ref, ps_ref, o_ref, c_ref, *, past_len):
    cwidth = cw_ref.shape[1]
    gb = rest_ref[:, 0 * cwidth:1 * cwidth]
    c = rest_ref[:, 1 * cwidth:2 * cwidth] * rest_ref[:, 2 * cwidth:3 * cwidth]
    u = rest_ref[:, 3 * cwidth:4 * cwidth]
    cw = cw_ref[...]
    conv = cw[0:1, :] * sc_ref[0] + cw[1:2, :] * sc_ref[1] + cw[2:3, :] * c
    c_ref[...] = c
    o_ref[:, :cwidth] = (gb * conv).astype(o_ref.dtype)

    group = cwidth // len(POOL_WINDOWS)
    sums, counts = [], []
    for g, win in enumerate(POOL_WINDOWS):
        cols = slice(g * group, (g + 1) * group)
        total = u[:, cols]
        for back in range(1, win):
            total = total + sp_ref[POOL_PAD - back][:, cols]
        sums.append(total)
        counts.append(float(min(win, past_len + 1)))
    o_ref[:, cwidth:] = _pool_mix(sums, u, counts, wp_ref, ps_ref[...]).astype(o_ref.dtype)


def _mixer_sample(rest, conv_state, pool_state, conv_w, w_pool, pool_scale, layer, *, past_len):
    n_seq, w4 = rest.shape
    cw = w4 // 4
    n_groups = len(POOL_WINDOWS)
    group = cw // n_groups
    return pl.pallas_call(
        functools.partial(_mixer_sample_body, past_len=past_len),
        grid=(1,),
        in_specs=[pl.BlockSpec((n_seq, w4), lambda i: (0, 0)),
                  pl.BlockSpec((CONV_K - 1, n_seq, cw), lambda i: (0, 0, 0)),
                  pl.BlockSpec((POOL_PAD, n_seq, cw), lambda i: (0, 0, 0)),
                  pl.BlockSpec((None, CONV_K, cw), lambda i: (layer, 0, 0)),
                  pl.BlockSpec((None, n_groups, group, group), lambda i: (layer, 0, 0, 0)),
                  pl.BlockSpec((None, 1, cw), lambda i: (layer, 0, 0))],
        out_specs=(pl.BlockSpec((n_seq, 2 * cw), lambda i: (0, 0)),
                   pl.BlockSpec((n_seq, cw), lambda i: (0, 0))),
        out_shape=(jax.ShapeDtypeStruct((n_seq, 2 * cw), BF16),
                   jax.ShapeDtypeStruct((n_seq, cw), F32)),
        compiler_params=_cparams(("arbitrary",)),
        name="mixer_sample",
    )(rest, conv_state, pool_state, conv_w, w_pool, pool_scale)


def _sample_layer(x, layer, p, sample):
    d = x.shape[1]
    att_w, cw = d // 2, d // 4
    f, f_pad = p['f'], p['f_pad']
    wts = {}

    xn = _rms_norm(x, p['norm_mix'], layer, BF16)
    proj, wts['in'] = _matmul_cast(xn, p['w_in'], layer, tn=1024)
    q, k, v, rest = (proj[:, :att_w], proj[:, att_w:2 * att_w], proj[:, 2 * att_w:3 * att_w], proj[:, 3 * att_w:])

    att = _attn_sample(q, sample['cache_k'], sample['cache_v'], sample['page_table'], p['sb_bias'], layer)
    conv_prev, pool_prev = sample['state_conv'][layer], sample['state_pool'][layer]
    mix, c_new = _mixer_sample(rest, conv_prev.transpose(1, 0, 2), pool_prev.transpose(1, 0, 2),
                               p['conv_w'], p['w_pool'], p['pool_scale'], layer, past_len=sample['past_len'])
    conv_state = jnp.concatenate([conv_prev[:, 1:], c_new[:, None]], axis=1)
    pool_state = jnp.concatenate([pool_prev[:, 1:], rest[:, None, 3 * cw:]], axis=1)

    mixed = jnp.concatenate([att, mix], axis=1)
    h, wts['out'] = _matmul_cast(mixed, p['w_out'], layer, residual=x, tn=1024)

    hn = _rms_norm(h, p['norm_ffn'], layer, BF16)
    ffn_prev = jnp.pad(sample['state_ffn'][layer], ((0, 0), (0, 0), (0, f_pad - f)))
    act, g_new, wts['gate'], wts['up'] = _ffn_sample(
        hn, p['w_ffn_gate'], p['w_ffn_up'], p['ffn_conv_w'], p['ffn_conv_b'], layer,
        state=(ffn_prev[:, 0], ffn_prev[:, 1]), f_pad=f_pad, tn=512)
    ffn_state = jnp.concatenate([ffn_prev[:, 1:], g_new[:, None]], axis=1)[:, :, :f]
    y, wts['down'] = _matmul_cast(act, p['w_ffn_down'], layer, k_len=f_pad // 2, residual=h, tn=512)
    return (y, k, v, conv_state, pool_state, ffn_state), wts


def _prompt_layer(x, layer, p, wts, kv_stack, *, n_seq, seq_len):
    d = x.shape[1]
    att_w, cw = d // 2, d // 4
    depth, f, half = p['depth'], p['f'], p['f_pad'] // 2
    tiles, res_tiles = dict(tm=1024, tn=1024), dict(tm=512, tn=1024)

    xn = _rms_norm(x, p['norm_mix'], layer, BF16)
    proj = functools.partial(_matmul, xn, wts['in'][None], 0, **tiles)
    q = proj(col0=0, n_out=att_w, out_dtype=BF16)
    k = proj(col0=att_w, n_out=att_w, out_dtype=F32, stack=(depth, layer, kv_stack[0]))
    v = proj(col0=2 * att_w, n_out=att_w, out_dtype=F32, stack=(depth, layer, kv_stack[1]))
    rest = proj(col0=3 * att_w, n_out=4 * cw, out_dtype=F32)

    att = _attn_prompt(q, k, v, p['sb_bias'], layer, n_seq=n_seq, seq_len=seq_len)
    mix, ctail, utail = _mixer_prompt(rest, p['conv_w'], p['w_pool'], p['pool_scale'], layer,
                                      n_seq=n_seq, seq_len=seq_len)
    conv_state = ctail[:, SUBLANES - (CONV_K - 1):]
    pool_state = utail[:, 2 * SUBLANES - POOL_PAD:]

    h = _matmul((att, mix), wts['out'][None], 0, col0=0, n_out=d, out_dtype=F32, residual=x, **res_tiles)

    hn = _rms_norm(h, p['norm_ffn'], layer, BF16)
    act, gtail = _ffn_gate_up(hn, wts['gate'], wts['up'], p['ffn_conv_w'], p['ffn_conv_b'], layer,
                              seq_len=seq_len, tm=1024, tn=512)
    ffn_state = gtail[:, SUBLANES - (CONV_K - 1):, :f]
    down = functools.partial(_matmul, act, wts['down'][None], 0, k_len=half, col0=0, n_out=d, out_dtype=F32,
                             **res_tiles)
    y = down(k0=half, residual=down(k0=0, residual=h))
    return y, (k, v), conv_state, pool_state, ffn_state


def kernel(x_prompt, x_sample, cache_k, cache_v, state_conv, state_pool, state_ffn, page_table, norm_mix, w_in,
           sb_bias, conv_w, w_pool, pool_scale, w_out, norm_ffn, w_ffn_gate, ffn_conv_w, ffn_conv_b, w_ffn_up,
           w_ffn_down, norm_final):
    bp, seq_len, d = x_prompt.shape
    bs, dec_seq, _ = x_sample.shape
    assert dec_seq == 1
    depth, n_phys, page, n_heads, head_dim = cache_k.shape
    assert head_dim == HEAD_DIM
    past_len = page_table.shape[1] * page
    f = w_ffn_gate.shape[2]
    f_pad = _cdiv(f, FFN_PAD) * FFN_PAD
    pad_f = ((0, 0), (0, 0), (0, f_pad - f))
    p = dict(
        depth=depth, f=f, f_pad=f_pad, norm_mix=norm_mix[:, None, :], sb_bias=sb_bias, conv_w=conv_w,
        w_pool=w_pool, pool_scale=pool_scale[:, None, :], norm_ffn=norm_ffn[:, None, :],
        ffn_conv_w=jnp.pad(ffn_conv_w, pad_f), ffn_conv_b=jnp.pad(ffn_conv_b[:, None, :], pad_f),
        w_in=w_in, w_out=w_out, w_ffn_gate=w_ffn_gate, w_ffn_up=w_ffn_up, w_ffn_down=w_ffn_down)
    sample = dict(
        cache_k=cache_k.reshape(depth, n_phys, page * n_heads, head_dim),
        cache_v=cache_v.reshape(depth, n_phys, page * n_heads, head_dim),
        page_table=page_table, state_conv=state_conv, state_pool=state_pool, state_ffn=state_ffn,
        past_len=past_len)

    yp = x_prompt.reshape(bp * seq_len, d)
    ys = x_sample.reshape(bs, d)
    kv_stack = (None, None)
    outs_p, outs_s = [], []
    for layer in range(depth):
        (ys, *rest_s), wts = _sample_layer(ys, layer, p, sample)
        yp, kv_stack, *rest_p = _prompt_layer(yp, layer, p, wts, kv_stack, n_seq=bp, seq_len=seq_len)
        outs_p.append(rest_p)
        outs_s.append(rest_s)
    final_gain = norm_final[None, None, :]
    yp = _rms_norm(yp, final_gain, 0, F32).reshape(bp, seq_len, d)
    ys = _rms_norm(ys, final_gain, 0, F32).reshape(bs, 1, d)

    def stack(outs, idx, shape):
        return jnp.stack([o[idx] for o in outs]).reshape(shape)

    kv_p = (depth, bp, seq_len, n_heads, head_dim)
    kv_s = (depth, bs, 1, n_heads, head_dim)
    return (yp, ys,
            kv_stack[0].reshape(kv_p), kv_stack[1].reshape(kv_p),
            stack(outs_p, 0, (depth, bp, CONV_K - 1, d // 4)),
            stack(outs_p, 1, (depth, bp, POOL_PAD, d // 4)),
            stack(outs_p, 2, (depth, bp, CONV_K - 1, -1)),
            stack(outs_s, 0, kv_s), stack(outs_s, 1, kv_s),
            stack(outs_s, 2, (depth, bs, CONV_K - 1, d // 4)),
            stack(outs_s, 3, (depth, bs, POOL_PAD, d // 4)),
            stack(outs_s, 4, (depth, bs, CONV_K - 1, -1)))
```
